```python
import math, functools
import jax, jax.numpy as jnp
from jax import lax
import numpy as np

D_MODEL = 2048
BATCH = 1
SEQ = 8192
DEPTH = 1
DEC_BATCH = 128
DEC_SEQ = 4
PAST_LEN = 2048
PAGE_SIZE = 128

CONV_WIDTH = D_MODEL // 2
CONV_K = 3
N_HEADS = 8
HEAD_DIM = 64
ATTN_WIDTH = N_HEADS * 2 * HEAD_DIM
N_EXPERTS = 32
TOP_K = 4
D_FF = D_MODEL
SWIGLU_ALPHA = 1.702
SWIGLU_LIMIT = 7.0
Q_BLOCK = 128
MOE_BLOCK = 128
NORM_EPS = 1e-6
NEG_INF = -1e30

kernel_name = 'hybrid_shortconv_diffattn_moe_step'


def rms_norm(x, g):
    xf = x.astype(jnp.float32)
    y = xf * lax.rsqrt(jnp.mean(xf * xf, axis=-1, keepdims=True) + NORM_EPS)
    return (y * g.astype(jnp.float32)).astype(x.dtype)


def alibi_slopes():
    return 2.0 ** (-8.0 * jnp.arange(1, N_HEADS + 1, dtype=jnp.float32) / N_HEADS)


def lambda_init(layer):
    return 0.8 - 0.6 * math.exp(-0.3 * layer)


def short_conv(u, prev, w):
    ext = jnp.concatenate([prev.astype(u.dtype), u], axis=1)
    L = u.shape[1]
    y = ext[:, 0:L] * w[0]
    for j in range(1, CONV_K):
        y = y + ext[:, j:j + L] * w[j]
    return y, ext[:, ext.shape[1] - (CONV_K - 1):]


def diff_attn_core(q, k, v, q_pos, k_pos, lam):
    s = jnp.einsum('bqhcd,bkhcd->bhcqk', q, k, preferred_element_type=jnp.float32) * (HEAD_DIM ** -0.5)
    dist = (q_pos[:, None] - k_pos[None, :]).astype(jnp.float32)
    s = s - alibi_slopes()[None, :, None, None, None] * dist
    s = jnp.where(k_pos[None, :] <= q_pos[:, None], s, NEG_INF)
    p = jax.nn.softmax(s, axis=-1)
    a = p[:, :, 0] - lam * p[:, :, 1]
    return jnp.einsum('bhqk,bkhe->bqhe', a.astype(v.dtype), v)


def attend_prompt(q, k, v, lam):
    n, s = q.shape[0], q.shape[1]
    nb = s // Q_BLOCK
    qb = jnp.swapaxes(q.reshape(n, nb, Q_BLOCK, N_HEADS, 2, HEAD_DIM), 0, 1)
    k_pos = jnp.arange(s, dtype=jnp.int32)

    def block(args):
        q_blk, i = args
        q_pos = i * Q_BLOCK + jnp.arange(Q_BLOCK, dtype=jnp.int32)
        return diff_attn_core(q_blk, k, v, q_pos, k_pos, lam)

    o = lax.map(block, (qb, jnp.arange(nb, dtype=jnp.int32)))
    return jnp.swapaxes(o, 0, 1).reshape(n, s, N_HEADS, 2 * HEAD_DIM)


def attend_paged(q, k, v, lam, cache_k, cache_v, page_table):
    n_new = q.shape[1]
    past = page_table.shape[1] * cache_k.shape[1]
    q_pos = past + jnp.arange(n_new, dtype=jnp.int32)
    k_pos = jnp.arange(past + n_new, dtype=jnp.int32)

    def one(args):
        q_i, k_i, v_i, pages = args
        k_all = jnp.concatenate([cache_k[pages].reshape(past, N_HEADS, 2, HEAD_DIM).astype(k_i.dtype), k_i], axis=0)
        v_all = jnp.concatenate([cache_v[pages].reshape(past, N_HEADS, 2 * HEAD_DIM).astype(v_i.dtype), v_i], axis=0)
        return diff_attn_core(q_i[None], k_all[None], v_all[None], q_pos, k_pos, lam)[0]

    return lax.map(one, (q, k, v, page_table))


def token_mixer(xn, conv_prev, attend, layer, p):
    n, L, _ = xn.shape
    proj = xn @ p['w_in']
    cuts = np.cumsum([CONV_WIDTH] * 3 + [ATTN_WIDTH] * 3 + [D_MODEL]).tolist()
    cb, cc, cx, q, k, v, ga, gb = jnp.split(proj, cuts, axis=-1)
    conv, conv_state = short_conv(cc * cx, conv_prev, p['w_conv'])
    y_a = cb * conv
    q = rms_norm(q.reshape(n, L, N_HEADS, 2, HEAD_DIM), p['q_norm_g'])
    k = rms_norm(k.reshape(n, L, N_HEADS, 2, HEAD_DIM), p['k_norm_g'])
    v = v.reshape(n, L, N_HEADS, 2 * HEAD_DIM)
    lam0 = lambda_init(layer)
    f32 = jnp.float32
    lam = (jnp.exp(jnp.sum(p['lambda_q1'].astype(f32) * p['lambda_k1'].astype(f32)))
           - jnp.exp(jnp.sum(p['lambda_q2'].astype(f32) * p['lambda_k2'].astype(f32))) + lam0)
    o = attend(q, k, v, lam)
    y_b = (rms_norm(o, p['subln_g']) * (1.0 - lam0)).reshape(n, L, ATTN_WIDTH)
    merged = jax.nn.sigmoid(ga) * (y_a @ p['w_proj_a']) + jax.nn.sigmoid(gb) * (y_b @ p['w_proj_b'])
    return merged @ p['w_o'], k, v, conv_state


def clamped_swiglu(h):
    x_glu = jnp.minimum(h[..., ::2], SWIGLU_LIMIT)
    x_lin = jnp.clip(h[..., 1::2], -SWIGLU_LIMIT, SWIGLU_LIMIT)
    return x_glu * jax.nn.sigmoid(SWIGLU_ALPHA * x_glu) * (x_lin + 1.0)


def moe(x2d, p):
    T = x2d.shape[0]
    logits = (x2d @ p['w_router']).astype(jnp.float32) + p['b_router'].astype(jnp.float32)
    top_vals, top_idx = lax.top_k(logits, TOP_K)
    gates = jax.nn.softmax(top_vals, axis=-1)
    N = T * TOP_K
    e_flat = top_idx.reshape(-1).astype(jnp.int32)
    g_flat = gates.reshape(-1)
    tok_flat = jnp.arange(N, dtype=jnp.int32) // TOP_K
    order = jnp.argsort(e_flat)
    e_sorted = e_flat[order]
    counts = jnp.bincount(e_flat, length=N_EXPERTS)
    starts = jnp.cumsum(counts) - counts
    padded = ((counts + MOE_BLOCK - 1) // MOE_BLOCK) * MOE_BLOCK
    pad_ends = jnp.cumsum(padded)
    pad_starts = pad_ends - padded
    dest = pad_starts[e_sorted] + (jnp.arange(N, dtype=jnp.int32) - starts[e_sorted])
    n_blocks = -(-N // MOE_BLOCK) + N_EXPERTS
    R = n_blocks * MOE_BLOCK
    row_tok = jnp.zeros((R,), jnp.int32).at[dest].set(tok_flat[order])
    row_gate = jnp.zeros((R,), jnp.float32).at[dest].set(g_flat[order])
    block_expert = jnp.minimum(
        jnp.searchsorted(pad_ends, jnp.arange(n_blocks, dtype=jnp.int32) * MOE_BLOCK, side='right'),
        N_EXPERTS - 1).astype(jnp.int32)
    xr = x2d[row_tok].reshape(n_blocks, MOE_BLOCK, D_MODEL)
    w1, b1, w2, b2 = p['w_mlp1'], p['b_mlp1'], p['w_mlp2'], p['b_mlp2']

    def expert_block(args):
        xb, e = args
        h = xb @ w1[e] + b1[e]
        return clamped_swiglu(h) @ w2[e] + b2[e]

    yr = lax.map(expert_block, (xr, block_expert)).reshape(R, D_MODEL)
    return jnp.zeros_like(x2d).at[row_tok].add(yr * row_gate[:, None].astype(yr.dtype))


def layer_forward(x, conv_prev, attend, layer, p):
    mix, k, v, conv_state = token_mixer(rms_norm(x, p['norm1_g']), conv_prev, attend, layer, p)
    h = x + mix
    ff = moe(rms_norm(h, p['norm2_g']).reshape(-1, D_MODEL), p)
    return h + ff.reshape(h.shape), k, v, conv_state


def setup_inputs(seed: int = 0) -> dict:
    key = jax.random.key(seed)
    ks = jax.random.split(key, 26)
    f32 = jnp.float32
    n_pages = PAST_LEN // PAGE_SIZE
    n_used = DEC_BATCH * n_pages
    n_phys = n_used + n_used // 4
    n_proj = 3 * CONV_WIDTH + 3 * ATTN_WIDTH + 2 * D_MODEL
    nrm = lambda k, shape, scale: jax.random.normal(k, shape, f32) * scale
    perm = jax.random.permutation(ks[5], n_phys).astype(jnp.int32)
    return {
        'x_prompt': nrm(ks[0], (BATCH, SEQ, D_MODEL), 1.0),
        'x_sample': nrm(ks[1], (DEC_BATCH, DEC_SEQ, D_MODEL), 1.0),
        'cache_k': nrm(ks[2], (DEPTH, n_phys, PAGE_SIZE, N_HEADS, 2, HEAD_DIM), 1.0),
        'cache_v': nrm(ks[3], (DEPTH, n_phys, PAGE_SIZE, N_HEADS, 2 * HEAD_DIM), 1.0),
        'state_conv': nrm(ks[4], (DEPTH, DEC_BATCH, CONV_K - 1, CONV_WIDTH), 1.0),
        'page_table': perm[:n_used].reshape(DEC_BATCH, n_pages),
        'norm1_g': 1.0 + nrm(ks[6], (DEPTH, D_MODEL), 0.02),
        'w_in': nrm(ks[7], (DEPTH, D_MODEL, n_proj), D_MODEL ** -0.5),
        'w_conv': nrm(ks[8], (DEPTH, CONV_K, CONV_WIDTH), 0.5),
        'q_norm_g': 1.0 + nrm(ks[9], (DEPTH, HEAD_DIM), 0.02),
        'k_norm_g': 1.0 + nrm(ks[10], (DEPTH, HEAD_DIM), 0.02),
        'lambda_q1': nrm(ks[11], (DEPTH, HEAD_DIM), 0.1),
        'lambda_k1': nrm(ks[12], (DEPTH, HEAD_DIM), 0.1),
        'lambda_q2': nrm(ks[13], (DEPTH, HEAD_DIM), 0.1),
        'lambda_k2': nrm(ks[14], (DEPTH, HEAD_DIM), 0.1),
        'subln_g': 1.0 + nrm(ks[15], (DEPTH, 2 * HEAD_DIM), 0.02),
        'w_proj_a': nrm(ks[16], (DEPTH, CONV_WIDTH, D_MODEL), CONV_WIDTH ** -0.5),
        'w_proj_b': nrm(ks[17], (DEPTH, ATTN_WIDTH, D_MODEL), ATTN_WIDTH ** -0.5),
        'w_o': nrm(ks[18], (DEPTH, D_MODEL, D_MODEL), D_MODEL ** -0.5),
        'norm2_g': 1.0 + nrm(ks[19], (DEPTH, D_MODEL), 0.02),
        'w_router': nrm(ks[20], (DEPTH, D_MODEL, N_EXPERTS), D_MODEL ** -0.5),
        'b_router': nrm(ks[21], (DEPTH, N_EXPERTS), 0.01),
        'w_mlp1': nrm(ks[22], (DEPTH, N_EXPERTS, D_MODEL, 2 * D_FF), D_MODEL ** -0.5),
        'b_mlp1': nrm(ks[23], (DEPTH, N_EXPERTS, 2 * D_FF), 0.01),
        'w_mlp2': nrm(ks[24], (DEPTH, N_EXPERTS, D_FF, D_MODEL), D_FF ** -0.5),
        'b_mlp2': nrm(ks[25], (DEPTH, N_EXPERTS, D_MODEL), 0.01),
    }


def reference(x_prompt, x_sample, cache_k, cache_v, state_conv, page_table,
              norm1_g, w_in, w_conv, q_norm_g, k_norm_g, lambda_q1, lambda_k1, lambda_q2, lambda_k2,
              subln_g, w_proj_a, w_proj_b, w_o, norm2_g, w_router, b_router,
              w_mlp1, b_mlp1, w_mlp2, b_mlp2):
    n_b = x_prompt.shape[0]
    h_p, h_s = x_prompt, x_sample
    kp_l, vp_l, cp_l, ks_l, vs_l, cs_l = [], [], [], [], [], []
    for l in range(DEPTH):
        p = {'norm1_g': norm1_g[l], 'w_in': w_in[l], 'w_conv': w_conv[l],
             'q_norm_g': q_norm_g[l], 'k_norm_g': k_norm_g[l],
             'lambda_q1': lambda_q1[l], 'lambda_k1': lambda_k1[l],
             'lambda_q2': lambda_q2[l], 'lambda_k2': lambda_k2[l],
             'subln_g': subln_g[l], 'w_proj_a': w_proj_a[l], 'w_proj_b': w_proj_b[l], 'w_o': w_o[l],
             'norm2_g': norm2_g[l], 'w_router': w_router[l], 'b_router': b_router[l],
             'w_mlp1': w_mlp1[l], 'b_mlp1': b_mlp1[l], 'w_mlp2': w_mlp2[l], 'b_mlp2': b_mlp2[l]}
        conv_zero = jnp.zeros((n_b, CONV_K - 1, CONV_WIDTH), h_p.dtype)
        h_p, k_p, v_p, c_p = layer_forward(h_p, conv_zero, attend_prompt, l, p)
        attend_s = functools.partial(attend_paged, cache_k=cache_k[l], cache_v=cache_v[l], page_table=page_table)
        h_s, k_s, v_s, c_s = layer_forward(h_s, state_conv[l], attend_s, l, p)
        kp_l.append(k_p); vp_l.append(v_p); cp_l.append(c_p)
        ks_l.append(k_s); vs_l.append(v_s); cs_l.append(c_s)
    k_prompt, v_prompt, conv_prompt = jnp.stack(kp_l), jnp.stack(vp_l), jnp.stack(cp_l)
    k_sample, v_sample, conv_sample = jnp.stack(ks_l), jnp.stack(vs_l), jnp.stack(cs_l)
    return (h_p, h_s, k_prompt, v_prompt, conv_prompt, k_sample, v_sample, conv_sample)
```

```python
import functools
import math

import jax
import jax.numpy as jnp
from jax import lax
from jax.experimental import pallas as pl
from jax.experimental.pallas import tpu as pltpu

F32 = jnp.float32
BF16 = jnp.bfloat16

TOP_K = 4
CONV_K = 3
NORM_EPS = 1e-6
NEG_INF = -1e30
SWIGLU_ALPHA = 1.702
SWIGLU_LIMIT = 7.0
LANES = 128
VMEM_LIMIT = 56 * 1024 * 1024


def _cparams(sem):
    return pltpu.CompilerParams(dimension_semantics=sem, vmem_limit_bytes=VMEM_LIMIT)


def _nt_dot(a, b):
    return lax.dot_general(a, b, (((1,), (1,)), ((), ())), preferred_element_type=F32)


def _dot(a, b):
    return jnp.dot(a, b, preferred_element_type=F32)


def _split_bf16(x):
    hi = x.astype(BF16)
    lo = (x - hi.astype(F32)).astype(BF16)
    return hi, lo


def _in_proj_kernel(x_ref, g_ref, w_ref, o_ref, xn_sc):
    @pl.when(pl.program_id(1) == 0)
    def _():
        x = x_ref[...]
        ms = jnp.mean(x * x, axis=-1, keepdims=True)
        xn_sc[...] = (x * lax.rsqrt(ms + NORM_EPS) * g_ref[...]).astype(BF16)

    o_ref[...] = _dot(xn_sc[...], w_ref[...])


def _in_proj(x, g, w_bf16, tm, tn):
    T, D = x.shape
    N = w_bf16.shape[1]
    return pl.pallas_call(
        _in_proj_kernel,
        grid=(T // tm, N // tn),
        in_specs=[
            pl.BlockSpec((tm, D), lambda i, j: (i, 0)),
            pl.BlockSpec((1, D), lambda i, j: (0, 0)),
            pl.BlockSpec((D, tn), lambda i, j: (0, j)),
        ],
        out_specs=pl.BlockSpec((tm, tn), lambda i, j: (i, j)),
        out_shape=jax.ShapeDtypeStruct((T, N), F32),
        scratch_shapes=[pltpu.VMEM((tm, D), BF16)],
        compiler_params=_cparams(("parallel", "arbitrary")),
        name="in_proj",
    )(x, g.reshape(1, D), w_bf16)


def _group_mean_sq(x, gmat):
    hi, lo = _split_bf16(x * x)
    outs = []
    for j in range(x.shape[1] // LANES):
        sl = slice(LANES * j, LANES * (j + 1))
        outs.append(_dot(hi[:, sl], gmat) + _dot(lo[:, sl], gmat))
    return jnp.concatenate(outs, axis=1)


def _mix_prep_kernel(cb_ref, cc_ref, cx_ref, ccp_ref, cxp_ref, q_ref, k_ref, v_ref,
                     m1_ref, m2_ref, s1_ref, s2_ref, wc_ref, gq_ref, gk_ref, gmat_ref,
                     ya_ref, u_ref, qz_ref, kf_ref, kb_ref, vb_ref, *, n_prompt_tiles, head_dim):
    i = pl.program_id(0)
    u = cc_ref[...] * cx_ref[...]
    u_ref[...] = u
    up = ccp_ref[...] * cxp_ref[...]
    row = lax.broadcasted_iota(jnp.int32, u.shape, 0)
    p1 = pltpu.roll(u, 1, 0)
    p1 = jnp.where(row == 0, up[7:8, :], p1)
    p2 = pltpu.roll(u, 2, 0)
    p2 = jnp.where(row == 0, up[6:7, :], jnp.where(row == 1, up[7:8, :], p2))
    is_sample = i >= n_prompt_tiles
    s1 = jnp.where(is_sample, s1_ref[...], 0.0)
    s2 = jnp.where(is_sample, s2_ref[...], 0.0)
    p1 = p1 * m1_ref[...] + s1
    p2 = p2 * m2_ref[...] + s2
    w = wc_ref[...]
    conv = p2 * w[0:1, :] + p1 * w[1:2, :] + u * w[2:3, :]
    ya_ref[...] = (cb_ref[...] * conv).astype(BF16)

    gmat = gmat_ref[...]
    q = q_ref[...]
    qn = q * lax.rsqrt(_group_mean_sq(q, gmat) + NORM_EPS) * gq_ref[...] * (head_dim ** -0.5)
    lane = lax.broadcasted_iota(jnp.int32, q.shape, 1)
    first_map = (lane // head_dim) % 2 == 0
    qz_ref[0] = jnp.where(first_map, qn, 0.0).astype(BF16)
    qz_ref[1] = jnp.where(first_map, 0.0, qn).astype(BF16)
    k = k_ref[...]
    kn = k * lax.rsqrt(_group_mean_sq(k, gmat) + NORM_EPS) * gk_ref[...]
    kf_ref[...] = kn
    kb_ref[...] = kn.astype(BF16)
    vb_ref[...] = v_ref[...].astype(BF16)


def _mix_prep(proj, m1, m2, s1, s2, w_conv, gq, gk, gmat, tm, n_prompt_tiles, head_dim):
    T = proj.shape[0]
    W = w_conv.shape[1]
    col = lambda c: pl.BlockSpec((tm, W), lambda i, c=c: (i, c))
    prev = lambda c: pl.BlockSpec((8, W), lambda i, c=c: (jnp.maximum(i * (tm // 8) - 1, 0), c))
    samp = pl.BlockSpec((tm, W), lambda i: (jnp.maximum(i - n_prompt_tiles, 0), 0))
    rowv = pl.BlockSpec((tm, 1), lambda i: (i, 0))
    full = lambda a: pl.BlockSpec(a.shape, lambda i: (0,) * a.ndim)
    out_row = pl.BlockSpec((tm, W), lambda i: (i, 0))
    return pl.pallas_call(
        functools.partial(_mix_prep_kernel, n_prompt_tiles=n_prompt_tiles, head_dim=head_dim),
        grid=(T // tm,),
        in_specs=[col(0), col(1), col(2), prev(1), prev(2), col(3), col(4), col(5),
                  rowv, rowv, samp, samp, full(w_conv), full(gq), full(gk), full(gmat)],
        out_specs=[out_row, out_row, pl.BlockSpec((2, tm, W), lambda i: (0, i, 0)),
                   out_row, out_row, out_row],
        out_shape=[jax.ShapeDtypeStruct((T, W), BF16), jax.ShapeDtypeStruct((T, W), F32),
                   jax.ShapeDtypeStruct((2, T, W), BF16), jax.ShapeDtypeStruct((T, W), F32),
                   jax.ShapeDtypeStruct((T, W), BF16), jax.ShapeDtypeStruct((T, W), BF16)],
        compiler_params=_cparams(("parallel",)),
        name="mix_prep",
    )(proj, proj, proj, proj, proj, proj, proj, proj, m1, m2, s1, s2, w_conv, gq, gk, gmat)


def _sub_norm(o, g, lam0):
    ms = jnp.mean(o * o, axis=-1, keepdims=True)
    return o * lax.rsqrt(ms + NORM_EPS) * g * (1.0 - lam0)


def _prompt_attn_kernel(slope_ref, lam_ref, qz_ref, k_ref, v_ref, g_ref, o_ref,
                        m_sc, l_sc, acc_sc, *, bq, bk, lam0):
    h, i, j = pl.program_id(0), pl.program_id(1), pl.program_id(2)
    last_j = (i * bq + bq - 1) // bk

    @pl.when(j == 0)
    def _():
        m_sc[...] = jnp.full(m_sc.shape, NEG_INF, F32)
        l_sc[...] = jnp.zeros(l_sc.shape, F32)
        acc_sc[...] = jnp.zeros(acc_sc.shape, F32)

    @pl.when(j <= last_j)
    def _():
        k = k_ref[...]
        v = v_ref[...]
        qpos = i * bq + lax.broadcasted_iota(jnp.int32, (bq, bk), 0)
        kpos = j * bk + lax.broadcasted_iota(jnp.int32, (bq, bk), 1)
        bias = slope_ref[h] * (qpos - kpos).astype(F32)
        visible = kpos <= qpos
        for c in range(2):
            s = jnp.where(visible, _nt_dot(qz_ref[c], k) - bias, NEG_INF)
            m_prev = m_sc[c]
            m_new = jnp.maximum(m_prev, jnp.max(s, axis=1, keepdims=True))
            p = jnp.exp(s - m_new)
            alpha = jnp.exp(m_prev - m_new)
            l_sc[c] = alpha * l_sc[c] + jnp.sum(p, axis=1, keepdims=True)
            acc_sc[c] = alpha * acc_sc[c] + _dot(p.astype(BF16), v)
            m_sc[c] = m_new

    @pl.when(j == last_j)
    def _():
        o = acc_sc[0] / l_sc[0] - lam_ref[0] * (acc_sc[1] / l_sc[1])
        o_ref[...] = _sub_norm(o, g_ref[...], lam0).astype(BF16)


def _prompt_attn(slopes, lam, qz, kb, vb, g_sub, n_rows, n_heads, bq, bk, lam0):
    hd2 = g_sub.shape[1]
    kv_idx = lambda h, i, j, *_: (jnp.minimum(j, (i * bq + bq - 1) // bk), h)
    return pl.pallas_call(
        functools.partial(_prompt_attn_kernel, bq=bq, bk=bk, lam0=lam0),
        grid_spec=pltpu.PrefetchScalarGridSpec(
            num_scalar_prefetch=2,
            grid=(n_heads, n_rows // bq, n_rows // bk),
            in_specs=[
                pl.BlockSpec((2, bq, hd2), lambda h, i, j, *_: (0, i, h)),
                pl.BlockSpec((bk, hd2), kv_idx),
                pl.BlockSpec((bk, hd2), kv_idx),
                pl.BlockSpec((1, hd2), lambda h, i, j, *_: (0, 0)),
            ],
            out_specs=pl.BlockSpec((bq, hd2), lambda h, i, j, *_: (i, h)),
            scratch_shapes=[pltpu.VMEM((2, bq, 1), F32), pltpu.VMEM((2, bq, 1), F32),
                            pltpu.VMEM((2, bq, hd2), F32)],
        ),
        out_shape=jax.ShapeDtypeStruct((n_rows, n_heads * hd2), BF16),
        compiler_params=_cparams(("parallel", "parallel", "arbitrary")),
        name="prompt_attn",
    )(slopes, lam, qz, kb, vb, g_sub)


QPAD = 8


def _paged_attn_kernel(pt_ref, lam_ref, q_ref, kc_ref, vc_ref, kn_ref, vn_ref, g_ref, o_ref,
                       qbd_sc, m_sc, l_sc, acc_sc, *, n_pages, page, n_new, n_heads, head_dim, lam0):
    p = pl.program_id(1)
    n_maps = 2 * n_heads
    rows = n_maps * QPAD
    width = n_maps * head_dim
    r = lax.broadcasted_iota(jnp.int32, (rows, 1), 0)
    qi = r % QPAD
    slope = jnp.exp2(-(r // (2 * QPAD) + 1).astype(F32))

    @pl.when(p == 0)
    def _():
        q = jnp.tile(q_ref[0], (n_maps, 1))
        rr = lax.broadcasted_iota(jnp.int32, (rows, width), 0)
        ll = lax.broadcasted_iota(jnp.int32, (rows, width), 1)
        qbd_sc[...] = jnp.where(ll // head_dim == rr // QPAD, q, 0.0).astype(BF16)
        m_sc[...] = jnp.full(m_sc.shape, NEG_INF, F32)
        l_sc[...] = jnp.zeros(l_sc.shape, F32)
        acc_sc[...] = jnp.zeros(acc_sc.shape, F32)

    def step(kblk, vblk, s_bias, visible):
        s = _nt_dot(qbd_sc[...], kblk) - s_bias
        if visible is not None:
            s = jnp.where(visible, s, NEG_INF)
        m_prev = m_sc[...]
        m_new = jnp.maximum(m_prev, jnp.max(s, axis=1, keepdims=True))
        pr = jnp.exp(s - m_new)
        alpha = jnp.exp(m_prev - m_new)
        l_sc[...] = alpha * l_sc[...] + jnp.sum(pr, axis=1, keepdims=True)
        acc_sc[...] = alpha * acc_sc[...] + _dot(pr.astype(BF16), vblk)
        m_sc[...] = m_new

    key = lax.broadcasted_iota(jnp.int32, (rows, page), 1)
    past = n_pages * page

    @pl.when(p < n_pages)
    def _():
        dist = (past + qi - (p * page + key)).astype(F32)
        step(kc_ref[0].astype(BF16), vc_ref[0].astype(BF16), slope * dist, None)

    @pl.when(p == n_pages)
    def _():
        zeros = jnp.zeros((page - QPAD, width), F32)
        kblk = jnp.concatenate([kn_ref[0], zeros], axis=0).astype(BF16)
        vblk = jnp.concatenate([vn_ref[0], zeros], axis=0).astype(BF16)
        dist = (qi - key).astype(F32)
        step(kblk, vblk, slope * dist, (key <= qi) & (key < n_new))

        c = r // QPAD
        wgt = jnp.where(c % 2 == 0, 1.0, -lam_ref[0]) / l_sc[...]
        rr = lax.broadcasted_iota(jnp.int32, (rows, width), 0)
        ll = lax.broadcasted_iota(jnp.int32, (rows, width), 1)
        own = ll // (2 * head_dim) == rr // (2 * QPAD)
        accw = jnp.where(own, acc_sc[...] * wgt, 0.0)
        o = jnp.sum(accw.reshape(n_maps, QPAD, width), axis=0)
        g = g_ref[...]
        hd2 = 2 * head_dim
        o_ref[0] = jnp.concatenate(
            [_sub_norm(o[:, hd2 * hh:hd2 * (hh + 1)], g, lam0) for hh in range(n_heads)],
            axis=1).astype(BF16)


def _paged_attn(page_table, lam, q_pad, cache_k, cache_v, k_new, v_new, g_sub,
                n_new, n_heads, head_dim, lam0):
    B, n_pages = page_table.shape
    _, page, width = cache_k.shape
    rows = 2 * n_heads * QPAD
    cache_idx = lambda b, p, pt, *_: (pt[b, jnp.minimum(p, n_pages - 1)], 0, 0)
    seq_blk = pl.BlockSpec((1, QPAD, width), lambda b, p, *_: (b, 0, 0))
    return pl.pallas_call(
        functools.partial(_paged_attn_kernel, n_pages=n_pages, page=page, n_new=n_new,
                          n_heads=n_heads, head_dim=head_dim, lam0=lam0),
        grid_spec=pltpu.PrefetchScalarGridSpec(
            num_scalar_prefetch=2,
            grid=(B, n_pages + 1),
            in_specs=[
                seq_blk,
                pl.BlockSpec((1, page, width), cache_idx),
                pl.BlockSpec((1, page, width), cache_idx),
                seq_blk, seq_blk,
                pl.BlockSpec((1, 2 * head_dim), lambda b, p, *_: (0, 0)),
            ],
            out_specs=seq_blk,
            scratch_shapes=[pltpu.VMEM((rows, width), BF16), pltpu.VMEM((rows, 1), F32),
                            pltpu.VMEM((rows, 1), F32), pltpu.VMEM((rows, width), F32)],
        ),
        out_shape=jax.ShapeDtypeStruct((B, QPAD, width), BF16),
        compiler_params=_cparams(("parallel", "arbitrary")),
        name="paged_attn",
    )(page_table, lam, q_pad, cache_k, cache_v, k_new, v_new, g_sub)


def _out_proj_kernel(ya_ref, yb_ref, ga_ref, gb_ref, x_ref, wa_ref, wb_ref, wo_ref, g2_ref,
                     wrh_ref, wrl_ref, br_ref, h_ref, xn_ref, ti_ref, tg_ref, *, top_k):
    pa = _dot(ya_ref[...], wa_ref[...])
    pb = _dot(yb_ref[...], wb_ref[...])
    merged = jax.nn.sigmoid(ga_ref[...]) * pa + jax.nn.sigmoid(gb_ref[...]) * pb
    h = x_ref[...] + _dot(merged.astype(BF16), wo_ref[...])
    h_ref[...] = h
    ms = jnp.mean(h * h, axis=-1, keepdims=True)
    xn = h * lax.rsqrt(ms + NORM_EPS) * g2_ref[...]
    xn_ref[...] = xn

    xh, xl = _split_bf16(xn)
    wrh = wrh_ref[...]
    lg = _nt_dot(wrh, xh) + _nt_dot(wrh, xl) + _nt_dot(wrl_ref[...], xh) + br_ref[...]
    n_exp = lg.shape[0]
    row = lax.broadcasted_iota(jnp.int32, lg.shape, 0)
    vals, idxs = [], []
    for _ in range(top_k):
        mx = jnp.max(lg, axis=0, keepdims=True)
        idx = jnp.min(jnp.where(lg == mx, row, n_exp), axis=0, keepdims=True)
        vals.append(mx)
        idxs.append(idx)
        lg = jnp.where(row == idx, -jnp.inf, lg)
    es = [jnp.exp(v - vals[0]) for v in vals]
    den = es[0]
    for e in es[1:]:
        den = den + e
    ti_ref[...] = jnp.concatenate(idxs, axis=0)
    tg_ref[...] = jnp.concatenate([e / den for e in es], axis=0)


def _out_proj(ya, yb, proj, x, wa, wb, wo, g2, wrh, wrl, br, tm, gate_col):
    T, D = x.shape
    W = ya.shape[1]
    E = wrh.shape[0]
    const = lambda a: pl.BlockSpec(a.shape, lambda i: (0,) * a.ndim, pipeline_mode=pl.Buffered(1))
    row = lambda w: pl.BlockSpec((tm, w), lambda i: (i, 0))
    return pl.pallas_call(
        functools.partial(_out_proj_kernel, top_k=TOP_K),
        grid=(T // tm,),
        in_specs=[row(W), row(W),
                  pl.BlockSpec((tm, D), lambda i: (i, gate_col)),
                  pl.BlockSpec((tm, D), lambda i: (i, gate_col + 1)),
                  row(D), const(wa), const(wb), const(wo), const(g2),
                  const(wrh), const(wrl), const(br)],
        out_specs=[row(D), row(D),
                   pl.BlockSpec((TOP_K, tm), lambda i: (0, i)),
                   pl.BlockSpec((TOP_K, tm), lambda i: (0, i))],
        out_shape=[jax.ShapeDtypeStruct((T, D), F32), jax.ShapeDtypeStruct((T, D), F32),
                   jax.ShapeDtypeStruct((TOP_K, T), jnp.int32),
                   jax.ShapeDtypeStruct((TOP_K, T), F32)],
        compiler_params=_cparams(("parallel",)),
        name="out_proj_router",
    )(ya, yb, proj, proj, x, wa, wb, wo, g2, wrh, wrl, br)


def _tiles_to_rows(ref, lead, n_sub):
    return jnp.concatenate([ref[lead + (slice(None), s, slice(None))] for s in range(n_sub)], axis=1)


def _moe_kernel(be_ref, bv_ref, tok_ref, x_hbm, w1_ref, b1_ref, w2_ref, b2_ref, g_ref, o_ref,
                xbuf, xb_sc, acc_sc, sem, *, bm, fc, n_sub):
    i, j = pl.program_id(0), pl.program_id(1)
    valid = bv_ref[i] > 0
    last = pl.num_programs(1) - 1

    def row_copy(r):
        return pltpu.make_async_copy(x_hbm.at[tok_ref[i * bm + r]], xbuf.at[r], sem.at[0])

    @pl.when(valid & (j == 0))
    def _():
        def start(r, c):
            row_copy(r).start()
            return c
        lax.fori_loop(0, bm, start, 0)

        def wait(r, c):
            row_copy(r).wait()
            return c
        lax.fori_loop(0, bm, wait, 0)
        xb_sc[...] = _tiles_to_rows(xbuf, (), n_sub).astype(BF16)
        acc_sc[...] = jnp.broadcast_to(b2_ref[0], acc_sc.shape)

    @pl.when(valid)
    def _():
        hcat = _dot(xb_sc[...], w1_ref[0, 0]) + b1_ref[0, 0]
        glu = jnp.minimum(hcat[:, :fc], SWIGLU_LIMIT)
        lin = jnp.clip(hcat[:, fc:], -SWIGLU_LIMIT, SWIGLU_LIMIT)
        act = glu * jax.nn.sigmoid(SWIGLU_ALPHA * glu) * (lin + 1.0)
        acc_sc[...] += _dot(act.astype(BF16), w2_ref[0])

    @pl.when(valid & (j == last))
    def _():
        y = acc_sc[...] * g_ref[...]
        for s in range(n_sub):
            o_ref[:, s, :] = y[:, LANES * s:LANES * (s + 1)]

    @pl.when(jnp.logical_not(valid) & (j == last))
    def _():
        o_ref[...] = jnp.zeros(o_ref.shape, F32)


def _moe(block_expert, block_valid, row_tok, x_tiles, w1c, b1c, w2, b2, row_gate, bm, fc):
    E, n_chunks, D, _ = w1c.shape
    n_sub = x_tiles.shape[1]
    n_blocks = block_expert.shape[0]
    last = n_chunks - 1
    chunk = lambda i, j, bv: j * bv[i] + last * (1 - bv[i])
    return pl.pallas_call(
        functools.partial(_moe_kernel, bm=bm, fc=fc, n_sub=n_sub),
        grid_spec=pltpu.PrefetchScalarGridSpec(
            num_scalar_prefetch=3,
            grid=(n_blocks, n_chunks),
            in_specs=[
                pl.BlockSpec(memory_space=pl.ANY),
                pl.BlockSpec((1, 1, D, 2 * fc), lambda i, j, be, bv, tk: (be[i], chunk(i, j, bv), 0, 0)),
                pl.BlockSpec((1, 1, 1, 2 * fc), lambda i, j, be, bv, tk: (be[i], chunk(i, j, bv), 0, 0)),
                pl.BlockSpec((1, fc, D), lambda i, j, be, bv, tk: (be[i], chunk(i, j, bv), 0)),
                pl.BlockSpec((1, 1, D), lambda i, j, be, bv, tk: (be[i], 0, 0)),
                pl.BlockSpec((bm, 1), lambda i, j, *_: (i, 0)),
            ],
            out_specs=pl.BlockSpec((bm, n_sub, LANES), lambda i, j, *_: (i, 0, 0)),
            scratch_shapes=[pltpu.VMEM((bm, n_sub, LANES), F32), pltpu.VMEM((bm, D), BF16),
                            pltpu.VMEM((bm, D), F32), pltpu.SemaphoreType.DMA((1,))],
        ),
        out_shape=jax.ShapeDtypeStruct((n_blocks * bm, n_sub, LANES), F32),
        compiler_params=_cparams(("arbitrary", "arbitrary")),
        name="moe_experts",
    )(block_expert, block_valid, row_tok, x_tiles, w1c, b1c, w2, b2, row_gate)


def _combine_kernel(pos_ref, y_hbm, h_ref, o_ref, buf, sem, *, tm, top_k, n_sub):
    i = pl.program_id(0)

    def row_copy(n):
        k, r = n // tm, n % tm
        return pltpu.make_async_copy(y_hbm.at[pos_ref[(i * tm + r) * top_k + k]], buf.at[k, r], sem.at[0])

    def start(n, c):
        row_copy(n).start()
        return c
    lax.fori_loop(0, top_k * tm, start, 0)

    def wait(n, c):
        row_copy(n).wait()
        return c
    lax.fori_loop(0, top_k * tm, wait, 0)

    acc = h_ref[...]
    for k in range(top_k):
        acc = acc + _tiles_to_rows(buf, (k,), n_sub)
    o_ref[...] = acc


def _combine(pos, y_tiles, h, tm):
    T, D = h.shape
    n_sub = y_tiles.shape[1]
    return pl.pallas_call(
        functools.partial(_combine_kernel, tm=tm, top_k=TOP_K, n_sub=n_sub),
        grid_spec=pltpu.PrefetchScalarGridSpec(
            num_scalar_prefetch=1,
            grid=(T // tm,),
            in_specs=[
                pl.BlockSpec(memory_space=pl.ANY),
                pl.BlockSpec((tm, D), lambda i, *_: (i, 0)),
            ],
            out_specs=pl.BlockSpec((tm, D), lambda i, *_: (i, 0)),
            scratch_shapes=[pltpu.VMEM((TOP_K, tm, n_sub, LANES), F32), pltpu.SemaphoreType.DMA((1,))],
        ),
        out_shape=jax.ShapeDtypeStruct((T, D), F32),
        compiler_params=_cparams(("arbitrary",)),
        name="moe_combine",
    )(pos, y_tiles, h)


def _routing(top_idx, top_gate, n_experts, bm):
    T = top_idx.shape[1]
    N = TOP_K * T
    e_flat = top_idx.reshape(-1)
    tok_flat = jnp.arange(N, dtype=jnp.int32) % T
    order = jnp.argsort(e_flat)
    e_sorted = e_flat[order]
    counts = jnp.bincount(e_flat, length=n_experts).astype(jnp.int32)
    starts = jnp.cumsum(counts) - counts
    padded = ((counts + bm - 1) // bm) * bm
    pad_ends = jnp.cumsum(padded)
    pad_starts = pad_ends - padded
    dest_sorted = pad_starts[e_sorted] + (jnp.arange(N, dtype=jnp.int32) - starts[e_sorted])
    n_blocks = -(-N // bm) + n_experts
    row_tok = jnp.zeros((n_blocks * bm,), jnp.int32).at[dest_sorted].set(tok_flat[order])
    row_gate = jnp.zeros((n_blocks * bm,), F32).at[dest_sorted].set(top_gate.reshape(-1)[order])
    dest =jnp.zeros((N,), jnp.int32).at[order].set(dest_sorted)
    pos = dest.reshape(TOP_K, T).T.reshape(-1)
    blk_start = jnp.arange(n_blocks, dtype=jnp.int32) * bm
    block_expert = jnp.minimum(jnp.searchsorted(pad_ends, blk_start, side='right'),
                               n_experts - 1).astype(jnp.int32)
    block_valid = (blk_start < pad_ends[-1]).astype(jnp.int32)
    last_expert = block_expert[jnp.maximum(pad_ends[-1] // bm - 1, 0)]
    block_expert = jnp.where(block_valid > 0, block_expert, last_expert)
    return row_tok, row_gate.reshape(-1, 1), pos, block_expert, block_valid


def _tile(n, pref):
    for t in pref:
        if n % t == 0:
            return t
    raise ValueError(f"no tile for {n}")


def _layer(layer, xp, xs, cache_k, cache_v, state_conv, page_table, p):
    n_b, seq, D = xp.shape
    dec_b, dec_seq, _ = xs.shape
    assert n_b == 1
    n_phys, page, n_heads, _, head_dim = cache_k.shape
    attn_w = n_heads * 2 * head_dim
    conv_w = p['w_conv'].shape[1]
    assert conv_w == attn_w
    n_exp = p['w_router'].shape[1]
    d_ff = p['w_mlp2'].shape[1]
    Tp, Ts = n_b * seq, dec_b * dec_seq
    T = Tp + Ts
    lam0 = 0.8 - 0.6 * math.exp(-0.3 * layer)

    x = jnp.concatenate([xp.reshape(Tp, D), xs.reshape(Ts, D)], axis=0)

    tm = _tile(math.gcd(Tp, Ts), (512, 256, 128))
    proj = _in_proj(x, p['norm1_g'], p['w_in'].astype(BF16), tm, 1024)

    tm2 = _tile(math.gcd(Tp, Ts), (256, 128))
    t_idx = jnp.arange(T, dtype=jnp.int32)
    pos_in_seq = jnp.where(t_idx < Tp, t_idx, (t_idx - Tp) % dec_seq)
    m1 = (pos_in_seq >= 1).astype(F32).reshape(T, 1)
    m2 = (pos_in_seq >= 2).astype(F32).reshape(T, 1)
    st = state_conv.astype(F32)
    zero = jnp.zeros((dec_b, dec_seq - 1, conv_w), F32)
    s1 = jnp.concatenate([st[:, 1:2], zero], axis=1).reshape(Ts, conv_w)
    s2 = jnp.concatenate([st[:, 0:2], zero[:, 1:]], axis=1).reshape(Ts, conv_w)
    reps = attn_w // head_dim
    gq = jnp.tile(p['q_norm_g'].astype(F32), reps).reshape(1, attn_w)
    gk = jnp.tile(p['k_norm_g'].astype(F32), reps).reshape(1, attn_w)
    lane = jnp.arange(LANES)
    gmat = ((lane[:, None] // head_dim == lane[None, :] // head_dim).astype(F32) / head_dim).astype(BF16)
    ya, u, qz, kf, kb, vb = _mix_prep(proj, m1, m2, s1, s2, p['w_conv'].astype(F32), gq, gk, gmat,
                                     tm2, Tp // tm2, head_dim)
    vf = proj[:, 3 * conv_w + 2 * attn_w:3 * conv_w + 3 * attn_w]

    f32 = F32
    lam = (jnp.exp(jnp.sum(p['lambda_q1'].astype(f32) * p['lambda_k1'].astype(f32)))
           - jnp.exp(jnp.sum(p['lambda_q2'].astype(f32) * p['lambda_k2'].astype(f32))) + lam0).reshape(1)
    slopes = 2.0 ** (-8.0 * jnp.arange(1, n_heads + 1, dtype=f32) / n_heads)
    g_sub = p['subln_g'].astype(F32).reshape(1, 2 * head_dim)
    bq = _tile(Tp, (512, 256, 128))
    yb_p = _prompt_attn(slopes, lam, qz, kb, vb, g_sub, Tp, n_heads, bq, bq, lam0)

    pad_rows = lambda a: jnp.pad(a.reshape(dec_b, dec_seq, attn_w), ((0, 0), (0, QPAD - dec_seq), (0, 0)))
    q_s = pad_rows((qz[0, Tp:] + qz[1, Tp:]).astype(F32))
    yb_s = _paged_attn(page_table, lam, q_s,
                       cache_k.reshape(n_phys, page, attn_w), cache_v.reshape(n_phys, page, attn_w),
                       pad_rows(kf[Tp:]), pad_rows(vf[Tp:]), g_sub, dec_seq, n_heads, head_dim, lam0)
    yb = jnp.concatenate([yb_p, yb_s[:, :dec_seq].reshape(Ts, attn_w)], axis=0)

    wr = p['w_router'].astype(F32).T
    wrh, wrl = _split_bf16(wr)
    h, xn2, top_idx, top_gate = _out_proj(
        ya, yb, proj, x, p['w_proj_a'].astype(BF16), p['w_proj_b'].astype(BF16), p['w_o'].astype(BF16),
        p['norm2_g'].astype(F32).reshape(1, D), wrh, wrl, p['b_router'].astype(F32).reshape(n_exp, 1),
        tm2, (3 * conv_w + 3 * attn_w) // D)

    bm, fc = 256, 512
    n_chunks = d_ff // fc
    row_tok, row_gate, pos, block_expert, block_valid = _routing(top_idx, top_gate, n_exp, bm)
    w1c = (p['w_mlp1'].reshape(n_exp, D, n_chunks, fc, 2).transpose(0, 2, 1, 4, 3)
           .reshape(n_exp, n_chunks, D, 2 * fc).astype(BF16))
    b1c = (p['b_mlp1'].astype(F32).reshape(n_exp, n_chunks, fc, 2).transpose(0, 1, 3, 2)
           .reshape(n_exp, n_chunks, 1, 2 * fc))
    yr = _moe(block_expert, block_valid, row_tok, xn2.reshape(T, D // LANES, LANES), w1c, b1c,
              p['w_mlp2'].astype(BF16), p['b_mlp2'].astype(F32).reshape(n_exp, 1, D), row_gate, bm, fc)
    y = _combine(pos, yr, h, 128)

    k_p = kf[:Tp].reshape(n_b, seq, n_heads, 2, head_dim)
    v_p = vf[:Tp].reshape(n_b, seq, n_heads, 2 * head_dim)
    c_p = u[Tp - (CONV_K - 1):Tp].reshape(n_b, CONV_K - 1, conv_w)
    k_s = kf[Tp:].reshape(dec_b, dec_seq, n_heads, 2, head_dim)
    v_s = vf[Tp:].reshape(dec_b, dec_seq, n_heads, 2 * head_dim)
    c_s = u[Tp:].reshape(dec_b, dec_seq, conv_w)[:, dec_seq - (CONV_K - 1):]
    return (y[:Tp].reshape(n_b, seq, D), y[Tp:].reshape(dec_b, dec_seq, D), k_p, v_p, c_p, k_s, v_s, c_s)


def kernel(x_prompt, x_sample, cache_k, cache_v, state_conv, page_table, norm1_g, w_in, w_conv, q_norm_g, k_norm_g, lambda_q1, lambda_k1, lambda_q2, lambda_k2, subln_g, w_proj_a, w_proj_b, w_o, norm2_g, w_router, b_router, w_mlp1, b_mlp1, w_mlp2, b_mlp2):
    depth = w_in.shape[0]
    names = ('norm1_g', 'w_in', 'w_conv', 'q_norm_g', 'k_norm_g', 'lambda_q1', 'lambda_k1', 'lambda_q2',
             'lambda_k2', 'subln_g', 'w_proj_a', 'w_proj_b', 'w_o', 'norm2_g', 'w_router', 'b_router',
             'w_mlp1', 'b_mlp1', 'w_mlp2', 'b_mlp2')
    stacked = (norm1_g, w_in, w_conv, q_norm_g, k_norm_g, lambda_q1, lambda_k1, lambda_q2, lambda_k2,
               subln_g, w_proj_a, w_proj_b, w_o, norm2_g, w_router, b_router, w_mlp1, b_mlp1, w_mlp2, b_mlp2)
    h_p, h_s = x_prompt, x_sample
    outs = [[] for _ in range(6)]
    for l in range(depth):
        p = {n: a[l] for n, a in zip(names, stacked)}
        res = _layer(l, h_p, h_s, cache_k[l], cache_v[l], state_conv[l], page_table, p)
        h_p, h_s = res[0], res[1]
        for acc, r in zip(outs, res[2:]):
            acc.append(r)
    return (h_p, h_s) + tuple(jnp.stack(o) for o in outs)
```

```python
import functools
import math

import jax
import jax.numpy as jnp
from jax import lax
from jax.experimental import pallas as pl
from jax.experimental.pallas import tpu as pltpu

F32 = jnp.float32
BF16 = jnp.bfloat16

TOP_K = 4
CONV_K = 3
NORM_EPS = 1e-6
NEG_INF = -1e30
SWIGLU_ALPHA = 1.702
SWIGLU_LIMIT = 7.0
LANES = 128
VMEM_LIMIT = 56 * 1024 * 1024


def _cparams(sem):
    return pltpu.CompilerParams(dimension_semantics=sem, vmem_limit_bytes=VMEM_LIMIT)


def _nt_dot(a, b):
    return lax.dot_general(a, b, (((1,), (1,)), ((), ())), preferred_element_type=F32)


def _dot(a, b):
    return jnp.dot(a, b, preferred_element_type=F32)


def _split_bf16(x):
    hi = x.astype(BF16)
    lo = (x - hi.astype(F32)).astype(BF16)
    return hi, lo


def _in_proj_kernel(x_ref, g_ref, w_ref, o_ref, xn_sc):
    @pl.when(pl.program_id(1) == 0)
    def _():
        x = x_ref[...]
        ms = jnp.mean(x * x, axis=-1, keepdims=True)
        xn_sc[...] = (x * lax.rsqrt(ms + NORM_EPS) * g_ref[...]).astype(BF16)

    o_ref[...] = _dot(xn_sc[...], w_ref[...])


def _in_proj(x, g, w_bf16, tm, tn):
    T, D = x.shape
    N = w_bf16.shape[1]
    return pl.pallas_call(
        _in_proj_kernel,
        grid=(T // tm, N // tn),
        in_specs=[
            pl.BlockSpec((tm, D), lambda i, j: (i, 0)),
            pl.BlockSpec((1, D), lambda i, j: (0, 0)),
            pl.BlockSpec((D, tn), lambda i, j: (0, j)),
        ],
        out_specs=pl.BlockSpec((tm, tn), lambda i, j: (i, j)),
        out_shape=jax.ShapeDtypeStruct((T, N), F32),
        scratch_shapes=[pltpu.VMEM((tm, D), BF16)],
        compiler_params=_cparams(("parallel", "arbitrary")),
        name="in_proj",
    )(x, g.reshape(1, D), w_bf16)


def _group_mean_sq(x, gmat):
    hi, lo = _split_bf16(x * x)
    outs = []
    for j in range(x.shape[1] // LANES):
        sl = slice(LANES * j, LANES * (j + 1))
        outs.append(_dot(hi[:, sl], gmat) + _dot(lo[:, sl], gmat))
    return jnp.concatenate(outs, axis=1)


def _mix_prep_kernel(cb_ref, cc_ref, cx_ref, ccp_ref, cxp_ref, q_ref, k_ref, v_ref,
                     m1_ref, m2_ref, s1_ref, s2_ref, wc_ref, gq_ref, gk_ref, gmat_ref,
                     ya_ref, u_ref, qz_ref, kf_ref, kb_ref, vb_ref, *, n_prompt_tiles, head_dim):
    i = pl.program_id(0)
    u = cc_ref[...] * cx_ref[...]
    u_ref[...] = u
    up = ccp_ref[...] * cxp_ref[...]
    row = lax.broadcasted_iota(jnp.int32, u.shape, 0)
    p1 = pltpu.roll(u, 1, 0)
    p1 = jnp.where(row == 0, up[7:8, :], p1)
    p2 = pltpu.roll(u, 2, 0)
    p2 = jnp.where(row == 0, up[6:7, :], jnp.where(row == 1, up[7:8, :], p2))
    is_sample = i >= n_prompt_tiles
    s1 = jnp.where(is_sample, s1_ref[...], 0.0)
    s2 = jnp.where(is_sample, s2_ref[...], 0.0)
    p1 = p1 * m1_ref[...] + s1
    p2 = p2 * m2_ref[...] + s2
    w = wc_ref[...]
    conv = p2 * w[0:1, :] + p1 * w[1:2, :] + u * w[2:3, :]
    ya_ref[...] = (cb_ref[...] * conv).astype(BF16)

    gmat = gmat_ref[...]
    q = q_ref[...]
    qn = q * lax.rsqrt(_group_mean_sq(q, gmat) + NORM_EPS) * gq_ref[...] * (head_dim ** -0.5)
    lane = lax.broadcasted_iota(jnp.int32, q.shape, 1)
    first_map = (lane // head_dim) % 2 == 0
    qz_ref[0] = jnp.where(first_map, qn, 0.0).astype(BF16)
    qz_ref[1] = jnp.where(first_map, 0.0, qn).astype(BF16)
    k = k_ref[...]
    kn = k * lax.rsqrt(_group_mean_sq(k, gmat) + NORM_EPS) * gk_ref[...]
    kf_ref[...] = kn
    kb_ref[...] = kn.astype(BF16)
    vb_ref[...] = v_ref[...].astype(BF16)


def _mix_prep(proj, m1, m2, s1, s2, w_conv, gq, gk, gmat, tm, n_prompt_tiles, head_dim):
    T = proj.shape[0]
    W = w_conv.shape[1]
    col = lambda c: pl.BlockSpec((tm, W), lambda i, c=c: (i, c))
    prev = lambda c: pl.BlockSpec((8, W), lambda i, c=c: (jnp.maximum(i * (tm // 8) - 1, 0), c))
    samp = pl.BlockSpec((tm, W), lambda i: (jnp.maximum(i - n_prompt_tiles, 0), 0))
    rowv = pl.BlockSpec((tm, 1), lambda i: (i, 0))
    full = lambda a: pl.BlockSpec(a.shape, lambda i: (0,) * a.ndim)
    out_row = pl.BlockSpec((tm, W), lambda i: (i, 0))
    return pl.pallas_call(
        functools.partial(_mix_prep_kernel, n_prompt_tiles=n_prompt_tiles, head_dim=head_dim),
        grid=(T // tm,),
        in_specs=[col(0), col(1), col(2), prev(1), prev(2), col(3), col(4), col(5),
                  rowv, rowv, samp, samp, full(w_conv), full(gq), full(gk), full(gmat)],
        out_specs=[out_row, out_row, pl.BlockSpec((2, tm, W), lambda i: (0, i, 0)),
                   out_row, out_row, out_row],
        out_shape=[jax.ShapeDtypeStruct((T, W), BF16), jax.ShapeDtypeStruct((T, W), F32),
                   jax.ShapeDtypeStruct((2, T, W), BF16), jax.ShapeDtypeStruct((T, W), F32),
                   jax.ShapeDtypeStruct((T, W), BF16), jax.ShapeDtypeStruct((T, W), BF16)],
        compiler_params=_cparams(("parallel",)),
        name="mix_prep",
    )(proj, proj, proj, proj, proj, proj, proj, proj, m1, m2, s1, s2, w_conv, gq, gk, gmat)


def _sub_norm(o, g, lam0):
    ms = jnp.mean(o * o, axis=-1, keepdims=True)
    return o * lax.rsqrt(ms + NORM_EPS) * g * (1.0 - lam0)


def _prompt_attn_kernel(slope_ref, lam_ref, qz_ref, k_ref, v_ref, g_ref, o_ref,
                        m_sc, l_sc, acc_sc, *, bq, bk, lam0):
    h, i, j = pl.program_id(0), pl.program_id(1), pl.program_id(2)
    last_j = (i * bq + bq - 1) // bk

    @pl.when(j == 0)
    def _():
        m_sc[...] = jnp.full(m_sc.shape, NEG_INF, F32)
        l_sc[...] = jnp.zeros(l_sc.shape, F32)
        acc_sc[...] = jnp.zeros(acc_sc.shape, F32)

    @pl.when(j <= last_j)
    def _():
        k = k_ref[...]
        v = v_ref[...]
        qpos = i * bq + lax.broadcasted_iota(jnp.int32, (bq, bk), 0)
        kpos = j * bk + lax.broadcasted_iota(jnp.int32, (bq, bk), 1)
        bias = slope_ref[h] * (qpos - kpos).astype(F32)
        visible = kpos <= qpos
        for c in range(2):
            s = jnp.where(visible, _nt_dot(qz_ref[c], k) - bias, NEG_INF)
            m_prev = m_sc[c]
            m_new = jnp.maximum(m_prev, jnp.max(s, axis=1, keepdims=True))
            p = jnp.exp(s - m_new)
            alpha = jnp.exp(m_prev - m_new)
            l_sc[c] = alpha * l_sc[c] + jnp.sum(p, axis=1, keepdims=True)
            acc_sc[c] = alpha * acc_sc[c] + _dot(p.astype(BF16), v)
            m_sc[c] = m_new

    @pl.when(j == last_j)
    def _():
        o = acc_sc[0] / l_sc[0] - lam_ref[0] * (acc_sc[1] / l_sc[1])
        o_ref[...] = _sub_norm(o, g_ref[...], lam0).astype(BF16)


def _prompt_attn(slopes, lam, qz, kb, vb, g_sub, n_rows, n_heads, bq, bk, lam0):
    hd2 = g_sub.shape[1]
    kv_idx = lambda h, i, j, *_: (jnp.minimum(j, (i * bq + bq - 1) // bk), h)
    return pl.pallas_call(
        functools.partial(_prompt_attn_kernel, bq=bq, bk=bk, lam0=lam0),
        grid_spec=pltpu.PrefetchScalarGridSpec(
            num_scalar_prefetch=2,
            grid=(n_heads, n_rows // bq, n_rows // bk),
            in_specs=[
                pl.BlockSpec((2, bq, hd2), lambda h, i, j, *_: (0, i, h)),
                pl.BlockSpec((bk, hd2), kv_idx),
                pl.BlockSpec((bk, hd2), kv_idx),
                pl.BlockSpec((1, hd2), lambda h, i, j, *_: (0, 0)),
            ],
            out_specs=pl.BlockSpec((bq, hd2), lambda h, i, j, *_: (i, h)),
            scratch_shapes=[pltpu.VMEM((2, bq, 1), F32), pltpu.VMEM((2, bq, 1), F32),
                            pltpu.VMEM((2, bq, hd2), F32)],
        ),
        out_shape=jax.ShapeDtypeStruct((n_rows, n_heads * hd2), BF16),
        compiler_params=_cparams(("parallel", "parallel", "arbitrary")),
        name="prompt_attn",
    )(slopes, lam, qz, kb, vb, g_sub)


QPAD = 8


def _paged_attn_kernel(pt_ref, lam_ref, q_ref, *rest, n_pages, pps, page, n_new, n_heads, head_dim, lam0):
    kc_refs, vc_refs = rest[:pps], rest[pps:2 * pps]
    kn_ref, vn_ref, g_ref, o_ref, qbd_sc, m_sc, l_sc, acc_sc = rest[2 * pps:]
    p = pl.program_id(1)
    n_steps = n_pages // pps
    n_maps = 2 * n_heads
    rows = n_maps * QPAD
    width = n_maps * head_dim
    r = lax.broadcasted_iota(jnp.int32, (rows, 1), 0)
    qi = r % QPAD
    slope = jnp.exp2(-(r // (2 * QPAD) + 1).astype(F32))

    @pl.when(p == 0)
    def _():
        q = jnp.tile(q_ref[0], (n_maps, 1))
        rr = lax.broadcasted_iota(jnp.int32, (rows, width), 0)
        ll = lax.broadcasted_iota(jnp.int32, (rows, width), 1)
        qbd_sc[...] = jnp.where(ll // head_dim == rr // QPAD, q, 0.0).astype(BF16)
        m_sc[...] = jnp.full(m_sc.shape, NEG_INF, F32)
        l_sc[...] = jnp.zeros(l_sc.shape, F32)
        acc_sc[...] = jnp.zeros(acc_sc.shape, F32)

    def step(s, vblk):
        m_prev = m_sc[...]
        m_new = jnp.maximum(m_prev, jnp.max(s, axis=1, keepdims=True))
        pr = jnp.exp(s - m_new)
        alpha = jnp.exp(m_prev - m_new)
        l_sc[...] = alpha * l_sc[...] + jnp.sum(pr, axis=1, keepdims=True)
        acc_sc[...] = alpha * acc_sc[...] + _dot(pr.astype(BF16), vblk)
        m_sc[...] = m_new

    past = n_pages * page

    @pl.when(p < n_steps)
    def _():
        key = lax.broadcasted_iota(jnp.int32, (rows, pps * page), 1)
        dist = (past + qi - (p * (pps * page) + key)).astype(F32)
        kt = jnp.concatenate([kc[0, 0].reshape(width, page).astype(BF16) for kc in kc_refs], axis=1)
        vblk = jnp.concatenate(
            [jnp.concatenate([vc[0, 0, :, hh, :] for hh in range(n_heads)], axis=1).astype(BF16)
             for vc in vc_refs], axis=0)
        step(_dot(qbd_sc[...], kt) - slope * dist, vblk)

    @pl.when(p == n_steps)
    def _():
        key = lax.broadcasted_iota(jnp.int32, (rows, page), 1)
        zeros = jnp.zeros((page - QPAD, width), F32)
        kblk = jnp.concatenate([kn_ref[0], zeros], axis=0).astype(BF16)
        vblk = jnp.concatenate([vn_ref[0], zeros], axis=0).astype(BF16)
        dist = (qi - key).astype(F32)
        s = _nt_dot(qbd_sc[...], kblk) - slope * dist
        step(jnp.where((key <= qi) & (key < n_new), s, NEG_INF), vblk)

        c = r // QPAD
        wgt = jnp.where(c % 2 == 0, 1.0, -lam_ref[0]) / l_sc[...]
        rr = lax.broadcasted_iota(jnp.int32, (rows, width), 0)
        ll = lax.broadcasted_iota(jnp.int32, (rows, width), 1)
        own = ll // (2 * head_dim) == rr // (2 * QPAD)
        accw = jnp.where(own, acc_sc[...] * wgt, 0.0)
        o = jnp.sum(accw.reshape(n_maps, QPAD, width), axis=0)
        g = g_ref[...]
        hd2 = 2 * head_dim
        o_ref[0] = jnp.concatenate(
            [_sub_norm(o[:, hd2 * hh:hd2 * (hh + 1)], g, lam0) for hh in range(n_heads)],
            axis=1).astype(BF16)


def _paged_attn(page_table, lam, q_pad, cache_kt, cache_v, layer, k_new, v_new, g_sub,
                n_new, n_heads, head_dim, lam0):
    B, n_pages = page_table.shape
    page = cache_v.shape[2]
    width = 2 * n_heads * head_dim
    rows = 2 * n_heads * QPAD
    pps = next(c for c in (4, 2, 1) if n_pages % c == 0)
    n_steps = n_pages // pps

    def page_of(b, p, pt, s):
        return pt[b, jnp.minimum(p, n_steps - 1) * pps + s]

    k_specs = [pl.BlockSpec((1, 1, n_heads, 2, head_dim, page),
                            lambda b, p, pt, *_, s=s: (layer, page_of(b, p, pt, s), 0, 0, 0, 0)) for s in range(pps)]
    v_specs = [pl.BlockSpec((1, 1, page, n_heads, 2 * head_dim),
                            lambda b, p, pt, *_, s=s: (layer, page_of(b, p, pt, s), 0, 0, 0)) for s in range(pps)]
    seq_blk = pl.BlockSpec((1, QPAD, width), lambda b, p, *_: (b, 0, 0))
    return pl.pallas_call(
        functools.partial(_paged_attn_kernel, n_pages=n_pages, pps=pps, page=page, n_new=n_new,
                          n_heads=n_heads, head_dim=head_dim, lam0=lam0),
        grid_spec=pltpu.PrefetchScalarGridSpec(
            num_scalar_prefetch=2,
            grid=(B, n_steps + 1),
            in_specs=[seq_blk] + k_specs + v_specs + [
                seq_blk, seq_blk,
                pl.BlockSpec((1, 2 * head_dim), lambda b, p, *_: (0, 0)),
            ],
            out_specs=seq_blk,
            scratch_shapes=[pltpu.VMEM((rows, width), BF16), pltpu.VMEM((rows, 1), F32),
                            pltpu.VMEM((rows, 1), F32), pltpu.VMEM((rows, width), F32)],
        ),
        out_shape=jax.ShapeDtypeStruct((B, QPAD, width), BF16),
        compiler_params=_cparams(("parallel", "arbitrary")),
        name="paged_attn",
    )(page_table, lam, q_pad, *([cache_kt] * pps), *([cache_v] * pps), k_new, v_new, g_sub)


def _out_proj_kernel(ya_ref, yb_ref, ga_ref, gb_ref, x_ref, wa_ref, wb_ref, wo_ref, g2_ref,
                     wrh_ref, wrl_ref, br_ref, h_ref, xn_ref, ti_ref, tg_ref, *, top_k):
    pa = _dot(ya_ref[...], wa_ref[...])
    pb = _dot(yb_ref[...], wb_ref[...])
    merged = jax.nn.sigmoid(ga_ref[...]) * pa + jax.nn.sigmoid(gb_ref[...]) * pb
    h = x_ref[...] + _dot(merged.astype(BF16), wo_ref[...])
    h_ref[...] = h
    ms = jnp.mean(h * h, axis=-1, keepdims=True)
    xn = h * lax.rsqrt(ms + NORM_EPS) * g2_ref[...]
    xn_ref[...] = xn

    xh, xl = _split_bf16(xn)
    wrh = wrh_ref[...]
    lg = _nt_dot(wrh, xh) + _nt_dot(wrh, xl) + _nt_dot(wrl_ref[...], xh) + br_ref[...]
    n_exp = lg.shape[0]
    row = lax.broadcasted_iota(jnp.int32, lg.shape, 0)
    vals, idxs = [], []
    for _ in range(top_k):
        mx = jnp.max(lg, axis=0, keepdims=True)
        idx = jnp.min(jnp.where(lg == mx, row, n_exp), axis=0, keepdims=True)
        vals.append(mx)
        idxs.append(idx)
        lg = jnp.where(row == idx, -jnp.inf, lg)
    es = [jnp.exp(v - vals[0]) for v in vals]
    den = es[0]
    for e in es[1:]:
        den = den + e
    ti_ref[...] = jnp.concatenate(idxs, axis=0)
    tg_ref[...] = jnp.concatenate([e / den for e in es], axis=0)


def _out_proj(ya, yb, proj, x, wa, wb, wo, g2, wrh, wrl, br, tm, gate_col):
    T, D = x.shape
    W = ya.shape[1]
    E = wrh.shape[0]
    const = lambda a: pl.BlockSpec(a.shape, lambda i: (0,) * a.ndim, pipeline_mode=pl.Buffered(1))
    row = lambda w: pl.BlockSpec((tm, w), lambda i: (i, 0))
    return pl.pallas_call(
        functools.partial(_out_proj_kernel, top_k=TOP_K),
        grid=(T // tm,),
        in_specs=[row(W), row(W),
                  pl.BlockSpec((tm, D), lambda i: (i, gate_col)),
                  pl.BlockSpec((tm, D), lambda i: (i, gate_col + 1)),
                  row(D), const(wa), const(wb), const(wo), const(g2),
                  const(wrh), const(wrl), const(br)],
        out_specs=[row(D), row(D),
                   pl.BlockSpec((TOP_K, tm), lambda i: (0, i)),
                   pl.BlockSpec((TOP_K, tm), lambda i: (0, i))],
        out_shape=[jax.ShapeDtypeStruct((T, D), F32), jax.ShapeDtypeStruct((T, D), F32),
                   jax.ShapeDtypeStruct((TOP_K, T), jnp.int32),
                   jax.ShapeDtypeStruct((TOP_K, T), F32)],
        compiler_params=_cparams(("parallel",)),
        name="out_proj_router",
    )(ya, yb, proj, proj, x, wa, wb, wo, g2, wrh, wrl, br)


def _tiles_to_rows(ref, lead, n_sub):
    return jnp.concatenate([ref[lead + (slice(None), s, slice(None))] for s in range(n_sub)], axis=1)


PAIR_W = 2 * LANES
CONV_ROWS = 256


def _moe_kernel(be_ref, bv_ref, bf_ref, nxe_ref, nxok_ref, tok_ref,
                x_hbm, w1_hbm, w2_hbm, perm_ref, b1_ref, b2_ref, g_ref, o_ref,
                xbuf, xb_sc, acc_sc, w1b, w2b, s1, s2, gsem, wsem, *, bm, fc, n_chunks, n_sub):
    i = pl.program_id(0)
    nb = pl.num_programs(0)
    valid = bv_ref[i] > 0
    e = be_ref[i]
    xslot = i % 2
    d_model = xb_sc.shape[1]

    def gather(blk, slot, start):
        def body(r, c):
            cp = pltpu.make_async_copy(x_hbm.at[tok_ref[blk * bm + r]], xbuf.at[slot, r], gsem.at[slot])
            if start:
                cp.start()
            else:
                cp.wait()
            return c
        lax.fori_loop(0, bm, body, 0)

    def weight_copies(ex, c, slot):
        col = pl.multiple_of(c * (2 * fc), 2 * fc)
        row = pl.multiple_of(c * fc, fc)
        return (pltpu.make_async_copy(w1_hbm.at[ex, :, pl.ds(col, 2 * fc)], s1.at[slot], wsem.at[0, slot]),
                pltpu.make_async_copy(w2_hbm.at[ex, pl.ds(row, fc), :], s2.at[slot], wsem.at[1, slot]))

    def weights_start(ex, c, slot):
        for cp in weight_copies(ex, c, slot):
            cp.start()

    def weights_wait(ex, c, slot):
        for cp in weight_copies(ex, c, slot):
            cp.wait()

    @pl.when(i == 0)
    def _():
        gather(0, 0, True)
        weights_start(e, 0, 0)

    def convert(c, slot):
        perm = perm_ref[...]
        for r in range(d_model // CONV_ROWS):
            rows = slice(r * CONV_ROWS, (r + 1) * CONV_ROWS)
            wt = s1[slot, rows, :].astype(BF16)
            for g in range(2 * fc // PAIR_W):
                cols = slice(g * PAIR_W, (g + 1) * PAIR_W)
                w1b[c, rows, cols] = _dot(wt[:, cols], perm).astype(BF16)
        w2b[c] = s2[slot].astype(BF16)

    def compute(c):
        col = pl.multiple_of(c * (2 * fc), 2 * fc)
        hcat = _dot(xb_sc[...], w1b[c]) + b1_ref[0, :, pl.ds(col, 2 * fc)]
        acts = []
        for g in range(2 * fc // PAIR_W):
            glu = jnp.minimum(hcat[:, g * PAIR_W:g * PAIR_W + LANES], SWIGLU_LIMIT)
            lin = jnp.clip(hcat[:, g * PAIR_W + LANES:(g + 1) * PAIR_W], -SWIGLU_LIMIT, SWIGLU_LIMIT)
            acts.append(glu * jax.nn.sigmoid(SWIGLU_ALPHA * glu) * (lin + 1.0))
        acc_sc[...] += _dot(jnp.concatenate(acts, axis=1).astype(BF16), w2b[c])

    @pl.when(valid)
    def _():
        nxt = jnp.minimum(i + 1, nb - 1)

        @pl.when((i + 1 < nb) & (bv_ref[nxt] > 0))
        def _():
            gather(i + 1, 1 - xslot, True)

        gather(i, xslot, False)
        xb_sc[...] = _tiles_to_rows(xbuf, (xslot,), n_sub).astype(BF16)
        acc_sc[...] = jnp.broadcast_to(b2_ref[0], acc_sc.shape)

        @pl.when(bf_ref[i] > 0)
        def _():
            def body(c, carry):
                slot = c % 2

                @pl.when(c + 1 < n_chunks)
                def _():
                    weights_start(e, c + 1, 1 - slot)

                @pl.when((c + 1 == n_chunks) & (nxok_ref[i] > 0))
                def _():
                    weights_start(nxe_ref[i], 0, 1 - slot)

                weights_wait(e, c, slot)
                convert(c, slot)
                compute(c)
                return carry
            lax.fori_loop(0, n_chunks, body, 0)

        @pl.when(bf_ref[i] == 0)
        def _():
            def body(c, carry):
                compute(c)
                return carry
            lax.fori_loop(0, n_chunks, body, 0)

        y = acc_sc[...] * g_ref[...]
        for s in range(n_sub):
            o_ref[:, s, :] = y[:, LANES * s:LANES * (s + 1)]

    @pl.when(jnp.logical_not(valid))
    def _():
        o_ref[...] = jnp.zeros(o_ref.shape, F32)


def _moe(route, x_tiles, w1, b1g, w2, b2, bm, fc):
    E, D, F2 = w1.shape
    n_chunks = F2 // (2 * fc)
    assert n_chunks % 2 == 0 and (2 * fc) % PAIR_W == 0 and D % CONV_ROWS == 0
    n_sub = x_tiles.shape[1]
    n_blocks = route['block_expert'].shape[0]
    col = jnp.arange(PAIR_W)
    src = jnp.where(col < LANES, 2 * col, 2 * (col - LANES) + 1)
    perm = (jnp.arange(PAIR_W)[:, None] == src[None, :]).astype(BF16)
    by_expert = lambda i, be, *_: (be[i], 0, 0)
    return pl.pallas_call(
        functools.partial(_moe_kernel, bm=bm, fc=fc, n_chunks=n_chunks, n_sub=n_sub),
        grid_spec=pltpu.PrefetchScalarGridSpec(
            num_scalar_prefetch=6,
            grid=(n_blocks,),
            in_specs=[
                pl.BlockSpec(memory_space=pl.ANY),
                pl.BlockSpec(memory_space=pl.ANY),
                pl.BlockSpec(memory_space=pl.ANY),
                pl.BlockSpec((PAIR_W, PAIR_W), lambda i, *_: (0, 0)),
                pl.BlockSpec((1, 1, F2), by_expert),
                pl.BlockSpec((1, 1, D), by_expert),
                pl.BlockSpec((bm, 1), lambda i, *_: (i, 0)),
            ],
            out_specs=pl.BlockSpec((bm, n_sub, LANES), lambda i, *_: (i, 0, 0)),
            scratch_shapes=[
                pltpu.VMEM((2, bm, n_sub, LANES), F32), pltpu.VMEM((bm, D), BF16), pltpu.VMEM((bm, D), F32),
                pltpu.VMEM((n_chunks, D, 2 * fc), BF16), pltpu.VMEM((n_chunks, fc, D), BF16),
                pltpu.VMEM((2, D, 2 * fc), F32), pltpu.VMEM((2, fc, D), F32),
                pltpu.SemaphoreType.DMA((2,)), pltpu.SemaphoreType.DMA((2, 2)),
            ],
        ),
        out_shape=jax.ShapeDtypeStruct((n_blocks * bm, n_sub, LANES), F32),
        compiler_params=_cparams(("arbitrary",)),
        name="moe_experts",
    )(route['block_expert'], route['block_valid'], route['block_first'], route['next_expert'],
      route['next_ok'], route['row_tok'], x_tiles, w1, w2, perm, b1g, b2, route['row_gate'])


def _combine_kernel(pos_ref, y_hbm, h_ref, o_ref, buf, sem, *, tm, top_k, n_sub):
    i = pl.program_id(0)
    slot = i % 2

    def gather(tile, sl, start):
        def body(n, c):
            cp = pltpu.make_async_copy(y_hbm.at[pos_ref[tile * (top_k * tm) + n]], buf.at[sl, n], sem.at[sl])
            if start:
                cp.start()
            else:
                cp.wait()
            return c
        lax.fori_loop(0, top_k * tm, body, 0)

    @pl.when(i == 0)
    def _():
        gather(0, 0, True)

    @pl.when(i + 1 < pl.num_programs(0))
    def _():
        gather(i + 1, 1 - slot, True)

    gather(i, slot, False)
    acc = h_ref[...]
    for k in range(top_k):
        acc = acc + jnp.concatenate([buf[slot, pl.ds(k * tm, tm), s, :] for s in range(n_sub)], axis=1)
    o_ref[...] = acc


def _combine(pos, y_tiles, h, tm):
    T, D = h.shape
    n_sub = y_tiles.shape[1]
    return pl.pallas_call(
        functools.partial(_combine_kernel, tm=tm, top_k=TOP_K, n_sub=n_sub),
        grid_spec=pltpu.PrefetchScalarGridSpec(
            num_scalar_prefetch=1,
            grid=(T // tm,),
            in_specs=[
                pl.BlockSpec(memory_space=pl.ANY),
                pl.BlockSpec((tm, D), lambda i, *_: (i, 0)),
            ],
            out_specs=pl.BlockSpec((tm, D), lambda i, *_: (i, 0)),
            scratch_shapes=[pltpu.VMEM((2, TOP_K * tm, n_sub, LANES), F32), pltpu.SemaphoreType.DMA((2,))],
        ),
        out_shape=jax.ShapeDtypeStruct((T, D), F32),
        compiler_params=_cparams(("arbitrary",)),
        name="moe_combine",
    )(pos, y_tiles, h)


def _routing(top_idx, top_gate, n_experts, bm, tc):
    T = top_idx.shape[1]
    N = TOP_K * T
    i32 = jnp.int32
    e_flat = top_idx.reshape(-1)
    iota = jnp.arange(N, dtype=i32)
    experts = jnp.arange(n_experts, dtype=i32)
    e_sorted, order = lax.sort_key_val(e_flat, iota)
    counts = jnp.sum((e_flat[:, None] == experts[None, :]).astype(i32), axis=0)
    starts = jnp.cumsum(counts) - counts
    padded = ((counts + bm - 1) // bm) * bm
    pad_ends = jnp.cumsum(padded)
    pad_starts = pad_ends - padded
    dest_sorted = pad_starts[e_sorted] + (iota - starts[e_sorted])
    _, dest = lax.sort_key_val(order, dest_sorted)
    pos = dest.reshape(TOP_K, T // tc, tc).transpose(1, 0, 2).reshape(-1)

    n_blocks = -(-N // bm) + n_experts
    blk_start = jnp.arange(n_blocks, dtype=i32) * bm
    blk_e = jnp.minimum(jnp.sum((pad_ends[None, :] <= blk_start[:, None]).astype(i32), axis=1), n_experts - 1)
    block_valid = (blk_start < pad_ends[-1]).astype(i32)
    prev_e = jnp.concatenate([jnp.full((1,), -1, i32), blk_e[:-1]])
    block_first = block_valid * (blk_e != prev_e).astype(i32)
    last_e = jnp.max(jnp.where(counts > 0, experts, 0))
    block_expert = jnp.where(block_valid > 0, blk_e, last_e)
    later = lax.cummin(jnp.where(counts > 0, experts, n_experts)[::-1])[::-1]
    nxt = jnp.concatenate([later[1:], jnp.full((1,), n_experts, i32)])[block_expert]
    next_ok = (nxt < n_experts).astype(i32)
    next_expert = jnp.minimum(nxt, n_experts - 1)

    row = jnp.arange(n_blocks * bm, dtype=i32)
    row_e = jnp.repeat(blk_e, bm)
    within = row - pad_starts[row_e]
    row_ok = (within < counts[row_e]) & (row < pad_ends[-1])
    src = jnp.clip(starts[row_e] + within, 0, N - 1)
    row_tok = jnp.where(row_ok, (order % T)[src], 0)
    row_gate = jnp.where(row_ok, top_gate.reshape(-1)[order][src], 0.0)
    return dict(row_tok=row_tok, row_gate=row_gate.reshape(-1, 1), pos=pos, block_expert=block_expert,
                block_valid=block_valid, block_first=block_first, next_expert=next_expert, next_ok=next_ok)


def _tile(n, pref):
    for t in pref:
        if n % t == 0:
            return t
    raise ValueError(f"no tile for {n}")


def _layer(layer, xp, xs, cache_k, cache_v, state_conv, page_table, p):
    n_b, seq, D = xp.shape
    dec_b, dec_seq, _ = xs.shape
    assert n_b == 1
    _, n_phys, page, n_heads, _, head_dim = cache_k.shape
    attn_w = n_heads * 2 * head_dim
    conv_w = p['w_conv'].shape[1]
    assert conv_w == attn_w
    n_exp = p['w_router'].shape[1]
    d_ff = p['w_mlp2'].shape[1]
    Tp, Ts = n_b * seq, dec_b * dec_seq
    T = Tp + Ts
    lam0 = 0.8 - 0.6 * math.exp(-0.3 * layer)

    x = jnp.concatenate([xp.reshape(Tp, D), xs.reshape(Ts, D)], axis=0)

    tm = _tile(math.gcd(Tp, Ts), (512, 256, 128))
    proj = _in_proj(x, p['norm1_g'], p['w_in'].astype(BF16), tm, 1024)

    tm2 = _tile(math.gcd(Tp, Ts), (256, 128))
    t_idx = jnp.arange(T, dtype=jnp.int32)
    pos_in_seq = jnp.where(t_idx < Tp, t_idx, (t_idx - Tp) % dec_seq)
    m1 = (pos_in_seq >= 1).astype(F32).reshape(T, 1)
    m2 = (pos_in_seq >= 2).astype(F32).reshape(T, 1)
    st = state_conv.astype(F32)
    zero = jnp.zeros((dec_b, dec_seq - 1, conv_w), F32)
    s1 = jnp.concatenate([st[:, 1:2], zero], axis=1).reshape(Ts, conv_w)
    s2 = jnp.concatenate([st[:, 0:2], zero[:, 1:]], axis=1).reshape(Ts, conv_w)
    reps = attn_w // head_dim
    gq = jnp.tile(p['q_norm_g'].astype(F32), reps).reshape(1, attn_w)
    gk = jnp.tile(p['k_norm_g'].astype(F32), reps).reshape(1, attn_w)
    lane = jnp.arange(LANES)
    gmat = ((lane[:, None] // head_dim == lane[None, :] // head_dim).astype(F32) / head_dim).astype(BF16)
    ya, u, qz, kf, kb, vb = _mix_prep(proj, m1, m2, s1, s2, p['w_conv'].astype(F32), gq, gk, gmat,
                                     tm2, Tp // tm2, head_dim)
    vf = proj[:, 3 * conv_w + 2 * attn_w:3 * conv_w + 3 * attn_w]

    f32 = F32
    lam = (jnp.exp(jnp.sum(p['lambda_q1'].astype(f32) * p['lambda_k1'].astype(f32)))
           - jnp.exp(jnp.sum(p['lambda_q2'].astype(f32) * p['lambda_k2'].astype(f32))) + lam0).reshape(1)
    slopes = 2.0 ** (-8.0 * jnp.arange(1, n_heads + 1, dtype=f32) / n_heads)
    g_sub = p['subln_g'].astype(F32).reshape(1, 2 * head_dim)
    bq = _tile(Tp, (512, 256, 128))
    yb_p = _prompt_attn(slopes, lam, qz, kb, vb, g_sub, Tp, n_heads, bq, bq, lam0)

    pad_rows = lambda a: jnp.pad(a.reshape(dec_b, dec_seq, attn_w), ((0, 0), (0, QPAD - dec_seq), (0, 0)))
    q_s = pad_rows((qz[0, Tp:] + qz[1, Tp:]).astype(F32))
    yb_s = _paged_attn(page_table, lam, q_s,
                       jnp.transpose(cache_k, (0, 1, 3, 4, 5, 2)), cache_v, layer,
                       pad_rows(kf[Tp:]), pad_rows(vf[Tp:]), g_sub, dec_seq, n_heads, head_dim, lam0)
    yb = jnp.concatenate([yb_p, yb_s[:, :dec_seq].reshape(Ts, attn_w)], axis=0)

    wr = p['w_router'].astype(F32).T
    wrh, wrl = _split_bf16(wr)
    h, xn2, top_idx, top_gate = _out_proj(
        ya, yb, proj, x, p['w_proj_a'].astype(BF16), p['w_proj_b'].astype(BF16), p['w_o'].astype(BF16),
        p['norm2_g'].astype(F32).reshape(1, D), wrh, wrl, p['b_router'].astype(F32).reshape(n_exp, 1),
        tm2, (3 * conv_w + 3 * attn_w) // D)

    bm, fc, tc = 256, 256, 128
    route = _routing(top_idx, top_gate, n_exp, bm, tc)
    b1g = (p['b_mlp1'].astype(F32).reshape(n_exp, 2 * d_ff // PAIR_W, LANES, 2).transpose(0, 1, 3, 2)
           .reshape(n_exp, 1, 2 * d_ff))
    yr = _moe(route, xn2.reshape(T, D // LANES, LANES), p['w_mlp1'].astype(F32), b1g,
              p['w_mlp2'].astype(F32), p['b_mlp2'].astype(F32).reshape(n_exp, 1, D), bm, fc)
    y = _combine(route['pos'], yr, h, tc)

    k_p = kf[:Tp].reshape(n_b, seq, n_heads, 2, head_dim)
    v_p = vf[:Tp].reshape(n_b, seq, n_heads, 2 * head_dim)
    c_p = u[Tp - (CONV_K - 1):Tp].reshape(n_b, CONV_K - 1, conv_w)
    k_s = kf[Tp:].reshape(dec_b, dec_seq, n_heads, 2, head_dim)
    v_s = vf[Tp:].reshape(dec_b, dec_seq, n_heads, 2 * head_dim)
    c_s = u[Tp:].reshape(dec_b, dec_seq, conv_w)[:, dec_seq - (CONV_K - 1):]
    return (y[:Tp].reshape(n_b, seq, D), y[Tp:].reshape(dec_b, dec_seq, D), k_p, v_p, c_p, k_s, v_s, c_s)


def kernel(x_prompt, x_sample, cache_k, cache_v, state_conv, page_table, norm1_g, w_in, w_conv, q_norm_g, k_norm_g, lambda_q1, lambda_k1, lambda_q2, lambda_k2, subln_g, w_proj_a, w_proj_b, w_o, norm2_g, w_router, b_router, w_mlp1, b_mlp1, w_mlp2, b_mlp2):
    depth = w_in.shape[0]
    names = ('norm1_g', 'w_in', 'w_conv', 'q_norm_g', 'k_norm_g', 'lambda_q1', 'lambda_k1', 'lambda_q2',
             'lambda_k2', 'subln_g', 'w_proj_a', 'w_proj_b', 'w_o', 'norm2_g', 'w_router', 'b_router',
             'w_mlp1', 'b_mlp1', 'w_mlp2', 'b_mlp2')
    stacked = (norm1_g, w_in, w_conv, q_norm_g, k_norm_g, lambda_q1, lambda_k1, lambda_q2, lambda_k2,
               subln_g, w_proj_a, w_proj_b, w_o, norm2_g, w_router, b_router, w_mlp1, b_mlp1, w_mlp2, b_mlp2)
    h_p, h_s = x_prompt, x_sample
    outs = [[] for _ in range(6)]
    for l in range(depth):
        p = {n: a[l] for n, a in zip(names, stacked)}
        res = _layer(l, h_p, h_s, cache_k, cache_v, state_conv[l], page_table, p)
        h_p, h_s = res[0], res[1]
        for acc, r in zip(outs, res[2:]):
            acc.append(r)
    return (h_p, h_s) + tuple(jnp.stack(o) for o in outs)
```

```python
import functools
import math

import jax
import jax.numpy as jnp
from jax import lax
from jax.experimental import pallas as pl
from jax.experimental.pallas import tpu as pltpu

F32 = jnp.float32
BF16 = jnp.bfloat16

TOP_K = 4
CONV_K = 3
NORM_EPS = 1e-6
NEG_INF = -1e30
SWIGLU_ALPHA = 1.702
SWIGLU_LIMIT = 7.0
LANES = 128
VMEM_LIMIT = 56 * 1024 * 1024


def _cparams(sem, flags=None):
    return pltpu.CompilerParams(dimension_semantics=sem, vmem_limit_bytes=VMEM_LIMIT, flags=flags)


def _nt_dot(a, b):
    return lax.dot_general(a, b, (((1,), (1,)), ((), ())), preferred_element_type=F32)


def _dot(a, b):
    return jnp.dot(a, b, preferred_element_type=F32)


def _split_bf16(x):
    hi = x.astype(BF16)
    lo = (x - hi.astype(F32)).astype(BF16)
    return hi, lo


def _in_proj_kernel(x_ref, g_ref, w_ref, o_ref, xn_sc):
    @pl.when(pl.program_id(1) == 0)
    def _():
        x = x_ref[...]
        ms = jnp.mean(x * x, axis=-1, keepdims=True)
        xn_sc[...] = (x * lax.rsqrt(ms + NORM_EPS) * g_ref[...]).astype(BF16)

    o_ref[...] = _dot(xn_sc[...], w_ref[...])


def _in_proj(x, g, w_bf16, tm, tn):
    T, D = x.shape
    N = w_bf16.shape[1]
    return pl.pallas_call(
        _in_proj_kernel,
        grid=(T // tm, N // tn),
        in_specs=[
            pl.BlockSpec((tm, D), lambda i, j: (i, 0)),
            pl.BlockSpec((1, D), lambda i, j: (0, 0)),
            pl.BlockSpec((D, tn), lambda i, j: (0, j)),
        ],
        out_specs=pl.BlockSpec((tm, tn), lambda i, j: (i, j)),
        out_shape=jax.ShapeDtypeStruct((T, N), F32),
        scratch_shapes=[pltpu.VMEM((tm, D), BF16)],
        compiler_params=_cparams(("parallel", "arbitrary")),
        name="in_proj",
    )(x, g.reshape(1, D), w_bf16)


def _group_mean_sq(x, gmat):
    hi, lo = _split_bf16(x * x)
    outs = []
    for j in range(x.shape[1] // LANES):
        sl = slice(LANES * j, LANES * (j + 1))
        outs.append(_dot(hi[:, sl], gmat) + _dot(lo[:, sl], gmat))
    return jnp.concatenate(outs, axis=1)


LOG2E = math.log2(math.e)
KPOS_SPLIT = 256


def _mix_prep_kernel(cb_ref, cc_ref, cx_ref, ccp_ref, cxp_ref, q_ref, k_ref, v_ref,
                     m1_ref, m2_ref, s1_ref, s2_ref, wc_ref, gq_ref, gk_ref, gmat_ref, qaug_ref,
                     ya_ref, u_ref, qa_ref, ka_ref, qs_ref, kf_ref, vb_ref,
                     *, n_prompt_tiles, head_dim, kv_block):
    i = pl.program_id(0)
    u = cc_ref[...] * cx_ref[...]
    u_ref[...] = u
    up = ccp_ref[...] * cxp_ref[...]
    row = lax.broadcasted_iota(jnp.int32, u.shape, 0)
    p1 = pltpu.roll(u, 1, 0)
    p1 = jnp.where(row == 0, up[7:8, :], p1)
    p2 = pltpu.roll(u, 2, 0)
    p2 = jnp.where(row == 0, up[6:7, :], jnp.where(row == 1, up[7:8, :], p2))
    is_sample = i >= n_prompt_tiles
    s1 = jnp.where(is_sample, s1_ref[...], 0.0)
    s2 = jnp.where(is_sample, s2_ref[...], 0.0)
    p1 = p1 * m1_ref[...] + s1
    p2 = p2 * m2_ref[...] + s2
    w = wc_ref[...]
    conv = p2 * w[0:1, :] + p1 * w[1:2, :] + u * w[2:3, :]
    ya_ref[...] = (cb_ref[...] * conv).astype(BF16)

    gmat = gmat_ref[...]
    q = q_ref[...]
    qn = q * lax.rsqrt(_group_mean_sq(q, gmat) + NORM_EPS) * gq_ref[...] * (head_dim ** -0.5)
    qs_ref[...] = qn
    k = k_ref[...]
    kn = k * lax.rsqrt(_group_mean_sq(k, gmat) + NORM_EPS) * gk_ref[...]
    kf_ref[...] = kn
    vb_ref[...] = v_ref[...].astype(BF16)

    lane = lax.broadcasted_iota(jnp.int32, q.shape, 1)
    off = lane % (2 * head_dim)
    first_map = off < head_dim
    pos = (i * q.shape[0] + lax.broadcasted_iota(jnp.int32, (q.shape[0], 1), 0)) % kv_block
    pos_lo = (pos % KPOS_SPLIT).astype(F32)
    pos_hi = (pos - pos % KPOS_SPLIT).astype(F32)
    ql = qn * LOG2E
    for c in range(2):
        data = first_map if c == 0 else jnp.logical_not(first_map)
        aug = off - (1 - c) * head_dim
        k_aug = jnp.where((aug == 0) | (aug == 2), pos_lo, jnp.where((aug == 1) | (aug == 3), pos_hi, 0.0))
        qa_ref[c] = jnp.where(data, ql, qaug_ref[c]).astype(BF16)
        ka_ref[c] = jnp.where(data, kn, k_aug).astype(BF16)


def _mix_prep(proj, m1, m2, s1, s2, w_conv, gq, gk, gmat, qaug, tm, n_prompt_tiles, head_dim, kv_block):
    T = proj.shape[0]
    W = w_conv.shape[1]
    col = lambda c: pl.BlockSpec((tm, W), lambda i, c=c: (i, c))
    prev = lambda c: pl.BlockSpec((8, W), lambda i, c=c: (jnp.maximum(i * (tm // 8) - 1, 0), c))
    samp = pl.BlockSpec((tm, W), lambda i: (jnp.maximum(i - n_prompt_tiles, 0), 0))
    rowv = pl.BlockSpec((tm, 1), lambda i: (i, 0))
    full = lambda a: pl.BlockSpec(a.shape, lambda i: (0,) * a.ndim)
    out_row = pl.BlockSpec((tm, W), lambda i: (i, 0))
    both_maps = pl.BlockSpec((2, tm, W), lambda i: (0, i, 0))
    return pl.pallas_call(
        functools.partial(_mix_prep_kernel, n_prompt_tiles=n_prompt_tiles, head_dim=head_dim, kv_block=kv_block),
        grid=(T // tm,),
        in_specs=[col(0), col(1), col(2), prev(1), prev(2), col(3), col(4), col(5),
                  rowv, rowv, samp, samp, full(w_conv), full(gq), full(gk), full(gmat), full(qaug)],
        out_specs=[out_row, out_row, both_maps, both_maps, out_row, out_row, out_row],
        out_shape=[jax.ShapeDtypeStruct((T, W), BF16), jax.ShapeDtypeStruct((T, W), F32),
                   jax.ShapeDtypeStruct((2, T, W), BF16), jax.ShapeDtypeStruct((2, T, W), BF16),
                   jax.ShapeDtypeStruct((T, W), F32), jax.ShapeDtypeStruct((T, W), F32),
                   jax.ShapeDtypeStruct((T, W), BF16)],
        compiler_params=_cparams(("parallel",)),
        name="mix_prep",
    )(proj, proj, proj, proj, proj, proj, proj, proj, m1, m2, s1, s2, w_conv, gq, gk, gmat, qaug)


def _sub_norm(o, g, lam0):
    ms = jnp.mean(o * o, axis=-1, keepdims=True)
    return o * lax.rsqrt(ms + NORM_EPS) * g * (1.0 - lam0)


ATTN_SUB = 512


def _prompt_attn_kernel(a_ref, lam_ref, ii_ref, jj_ref, qa_ref, ka_ref, v_ref, g_ref, o_ref,
                        m_sc, acc_sc, *, blk, sub, lam0):
    h, t = pl.program_id(0), pl.program_id(1)
    i, j = ii_ref[t], jj_ref[t]
    hd2 = v_ref.shape[1]

    @pl.when(j == 0)
    def _():
        m_sc[...] = jnp.full(m_sc.shape, NEG_INF, F32)
        acc_sc[...] = jnp.zeros(acc_sc.shape, F32)

    v_ones = jnp.concatenate([v_ref[...], jnp.ones((blk, hd2), BF16)], axis=1)

    def update(masked):
        for r0 in range(0, blk, sub):
            rows = pl.ds(r0, sub)
            row = r0 + lax.broadcasted_iota(jnp.int32, (sub, 1), 0)
            row_term = a_ref[h] * ((i - j) * blk + row).astype(F32)
            for c in range(2):
                s = _nt_dot(qa_ref[c, rows, :], ka_ref[c])
                if masked:
                    s = jnp.where(lax.broadcasted_iota(jnp.int32, s.shape, 1) <= row, s, NEG_INF)
                m_prev = m_sc[c, rows, :]
                m_new = jnp.maximum(m_prev, jnp.max(s, axis=1, keepdims=True) - row_term)
                p = jnp.exp2(s - (row_term + m_new)).astype(BF16)
                acc_sc[c, rows, :] = jnp.exp2(m_prev - m_new) * acc_sc[c, rows, :] + _dot(p, v_ones)
                m_sc[c, rows, :] = m_new

    @pl.when(j < i)
    def _():
        update(False)

    @pl.when(j == i)
    def _():
        update(True)
        a0, a1 = acc_sc[0], acc_sc[1]
        o = a0[:, :hd2] / a0[:, hd2:] - lam_ref[0] * (a1[:, :hd2] / a1[:, hd2:])
        o_ref[...] = _sub_norm(o, g_ref[...], lam0).astype(BF16)


def _prompt_attn(a_log2, lam, qa, ka, vb, g_sub, n_rows, n_heads, blk, lam0):
    hd2 = g_sub.shape[1]
    nb = n_rows // blk
    pairs = [(i, j) for i in range(nb) for j in range(i + 1)]
    ii = jnp.asarray([p[0] for p in pairs], jnp.int32)
    jj = jnp.asarray([p[1] for p in pairs], jnp.int32)
    return pl.pallas_call(
        functools.partial(_prompt_attn_kernel, blk=blk, sub=min(blk, ATTN_SUB), lam0=lam0),
        grid_spec=pltpu.PrefetchScalarGridSpec(
            num_scalar_prefetch=4,
            grid=(n_heads, len(pairs)),
            in_specs=[
                pl.BlockSpec((2, blk, hd2), lambda h, t, a, l, ii, jj: (0, ii[t], h)),
                pl.BlockSpec((2, blk, hd2), lambda h, t, a, l, ii, jj: (0, jj[t], h)),
                pl.BlockSpec((blk, hd2), lambda h, t, a, l, ii, jj: (jj[t], h)),
                pl.BlockSpec((1, hd2), lambda h, t, *_: (0, 0)),
            ],
            out_specs=pl.BlockSpec((blk, hd2), lambda h, t, a, l, ii, jj: (ii[t], h)),
            scratch_shapes=[pltpu.VMEM((2, blk, 1), F32), pltpu.VMEM((2, blk, 2 * hd2), F32)],
        ),
        out_shape=jax.ShapeDtypeStruct((n_rows, n_heads * hd2), BF16),
        compiler_params=_cparams(("parallel", "arbitrary")),
        name="prompt_attn",
    )(a_log2, lam, ii, jj, qa, ka, vb, g_sub)


QPAD = 8


def _paged_attn_kernel(pt_ref, lam_ref, q_ref, *rest, n_pages, pps, page, n_new, n_heads, head_dim, lam0):
    kc_refs, vc_refs = rest[:pps], rest[pps:2 * pps]
    kn_ref, vn_ref, g_ref, o_ref, qbd_sc, m_sc, l_sc, acc_sc = rest[2 * pps:]
    p = pl.program_id(1)
    n_steps = n_pages // pps
    n_maps = 2 * n_heads
    rows = n_maps * QPAD
    width = n_maps * head_dim
    r = lax.broadcasted_iota(jnp.int32, (rows, 1), 0)
    qi = r % QPAD
    slope = jnp.exp2(-(r // (2 * QPAD) + 1).astype(F32))

    @pl.when(p == 0)
    def _():
        q = jnp.tile(q_ref[0], (n_maps, 1))
        rr = lax.broadcasted_iota(jnp.int32, (rows, width), 0)
        ll = lax.broadcasted_iota(jnp.int32, (rows, width), 1)
        qbd_sc[...] = jnp.where(ll // head_dim == rr // QPAD, q, 0.0).astype(BF16)
        m_sc[...] = jnp.full(m_sc.shape, NEG_INF, F32)
        l_sc[...] = jnp.zeros(l_sc.shape, F32)
        acc_sc[...] = jnp.zeros(acc_sc.shape, F32)

    def step(s, vblk):
        m_prev = m_sc[...]
        m_new = jnp.maximum(m_prev, jnp.max(s, axis=1, keepdims=True))
        pr = jnp.exp(s - m_new)
        alpha = jnp.exp(m_prev - m_new)
        l_sc[...] = alpha * l_sc[...] + jnp.sum(pr, axis=1, keepdims=True)
        acc_sc[...] = alpha * acc_sc[...] + _dot(pr.astype(BF16), vblk)
        m_sc[...] = m_new

    past = n_pages * page

    @pl.when(p < n_steps)
    def _():
        key = lax.broadcasted_iota(jnp.int32, (rows, pps * page), 1)
        dist = (past + qi - (p * (pps * page) + key)).astype(F32)
        kt = jnp.concatenate([kc[0, 0].reshape(width, page).astype(BF16) for kc in kc_refs], axis=1)
        vblk = jnp.concatenate(
            [jnp.concatenate([vc[0, 0, pl.ds(hh, page, stride=n_heads), :] for hh in range(n_heads)],
                             axis=1).astype(BF16) for vc in vc_refs], axis=0)
        step(_dot(qbd_sc[...], kt) - slope * dist, vblk)

    @pl.when(p == n_steps)
    def _():
        key = lax.broadcasted_iota(jnp.int32, (rows, page), 1)
        zeros = jnp.zeros((page - QPAD, width), F32)
        kblk = jnp.concatenate([kn_ref[0], zeros], axis=0).astype(BF16)
        vblk = jnp.concatenate([vn_ref[0], zeros], axis=0).astype(BF16)
        dist = (qi - key).astype(F32)
        s = _nt_dot(qbd_sc[...], kblk) - slope * dist
        step(jnp.where((key <= qi) & (key < n_new), s, NEG_INF), vblk)

        c = r // QPAD
        wgt = jnp.where(c % 2 == 0, 1.0, -lam_ref[0]) / l_sc[...]
        rr = lax.broadcasted_iota(jnp.int32, (rows, width), 0)
        ll = lax.broadcasted_iota(jnp.int32, (rows, width), 1)
        own = ll // (2 * head_dim) == rr // (2 * QPAD)
        accw = jnp.where(own, acc_sc[...] * wgt, 0.0)
        o = jnp.sum(accw.reshape(n_maps, QPAD, width), axis=0)
        g = g_ref[...]
        hd2 = 2 * head_dim
        o_ref[0] = jnp.concatenate(
            [_sub_norm(o[:, hd2 * hh:hd2 * (hh + 1)], g, lam0) for hh in range(n_heads)],
            axis=1).astype(BF16)


def _paged_attn(page_table, lam, q_pad, cache_kt, cache_v, layer, k_new, v_new, g_sub,
                n_new, n_heads, head_dim, lam0):
    B, n_pages = page_table.shape
    page = cache_v.shape[2]
    width = 2 * n_heads * head_dim
    rows = 2 * n_heads * QPAD
    pps = next(c for c in (4, 2, 1) if n_pages % c == 0)
    n_steps = n_pages // pps

    def page_of(b, p, pt, s):
        return pt[b, jnp.minimum(p, n_steps - 1) * pps + s]

    k_specs = [pl.BlockSpec((1, 1, n_heads, 2, head_dim, page),
                            lambda b, p, pt, *_, s=s: (layer, page_of(b, p, pt, s), 0, 0, 0, 0)) for s in range(pps)]
    v_rows = cache_v.reshape(cache_v.shape[0], cache_v.shape[1], page * n_heads, 2 * head_dim)
    v_specs = [pl.BlockSpec((1, 1, page * n_heads, 2 * head_dim),
                            lambda b, p, pt, *_, s=s: (layer, page_of(b, p, pt, s), 0, 0)) for s in range(pps)]
    seq_blk = pl.BlockSpec((1, QPAD, width), lambda b, p, *_: (b, 0, 0))
    return pl.pallas_call(
        functools.partial(_paged_attn_kernel, n_pages=n_pages, pps=pps, page=page, n_new=n_new,
                          n_heads=n_heads, head_dim=head_dim, lam0=lam0),
        grid_spec=pltpu.PrefetchScalarGridSpec(
            num_scalar_prefetch=2,
            grid=(B, n_steps + 1),
            in_specs=[seq_blk] + k_specs + v_specs + [
                seq_blk, seq_blk,
                pl.BlockSpec((1, 2 * head_dim), lambda b, p, *_: (0, 0)),
            ],
            out_specs=seq_blk,
            scratch_shapes=[pltpu.VMEM((rows, width), BF16), pltpu.VMEM((rows, 1), F32),
                            pltpu.VMEM((rows, 1), F32), pltpu.VMEM((rows, width), F32)],
        ),
        out_shape=jax.ShapeDtypeStruct((B, QPAD, width), BF16),
        compiler_params=_cparams(("parallel", "arbitrary")),
        name="paged_attn",
    )(page_table, lam, q_pad, *([cache_kt] * pps), *([v_rows] * pps), k_new, v_new, g_sub)


def _out_proj_kernel(ya_ref, yb_ref, ga_ref, gb_ref, x_ref, wa_ref, wb_ref, wo_ref, g2_ref,
                     wrh_ref, wrl_ref, br_ref, h_ref, xn_ref, ti_ref, tg_ref, *, top_k):
    pa = _dot(ya_ref[...], wa_ref[...])
    pb = _dot(yb_ref[...], wb_ref[...])
    merged = jax.nn.sigmoid(ga_ref[...]) * pa + jax.nn.sigmoid(gb_ref[...]) * pb
    h = x_ref[...] + _dot(merged.astype(BF16), wo_ref[...])
    h_ref[...] = h
    ms = jnp.mean(h * h, axis=-1, keepdims=True)
    xn = h * lax.rsqrt(ms + NORM_EPS) * g2_ref[...]
    xn_ref[...] = xn

    xh, xl = _split_bf16(xn)
    wrh = wrh_ref[...]
    lg = _nt_dot(wrh, xh) + _nt_dot(wrh, xl) + _nt_dot(wrl_ref[...], xh) + br_ref[...]
    n_exp = lg.shape[0]
    row = lax.broadcasted_iota(jnp.int32, lg.shape, 0)
    vals, idxs = [], []
    for _ in range(top_k):
        mx = jnp.max(lg, axis=0, keepdims=True)
        idx = jnp.min(jnp.where(lg == mx, row, n_exp), axis=0, keepdims=True)
        vals.append(mx)
        idxs.append(idx)
        lg = jnp.where(row == idx, -jnp.inf, lg)
    es = [jnp.exp(v - vals[0]) for v in vals]
    den = es[0]
    for e in es[1:]:
        den = den + e
    ti_ref[...] = jnp.concatenate(idxs, axis=0)
    tg_ref[...] = jnp.concatenate([e / den for e in es], axis=0)


def _out_proj(ya, yb, proj, x, wa, wb, wo, g2, wrh, wrl, br, tm, gate_col):
    T, D = x.shape
    W = ya.shape[1]
    E = wrh.shape[0]
    const = lambda a: pl.BlockSpec(a.shape, lambda i: (0,) * a.ndim, pipeline_mode=pl.Buffered(1))
    row = lambda w: pl.BlockSpec((tm, w), lambda i: (i, 0))
    return pl.pallas_call(
        functools.partial(_out_proj_kernel, top_k=TOP_K),
        grid=(T // tm,),
        in_specs=[row(W), row(W),
                  pl.BlockSpec((tm, D), lambda i: (i, gate_col)),
                  pl.BlockSpec((tm, D), lambda i: (i, gate_col + 1)),
                  row(D), const(wa), const(wb), const(wo), const(g2),
                  const(wrh), const(wrl), const(br)],
        out_specs=[row(D), row(D),
                   pl.BlockSpec((TOP_K, tm), lambda i: (0, i)),
                   pl.BlockSpec((TOP_K, tm), lambda i: (0, i))],
        out_shape=[jax.ShapeDtypeStruct((T, D), F32), jax.ShapeDtypeStruct((T, D), F32),
                   jax.ShapeDtypeStruct((TOP_K, T), jnp.int32),
                   jax.ShapeDtypeStruct((TOP_K, T), F32)],
        compiler_params=_cparams(("parallel",)),
        name="out_proj_router",
    )(ya, yb, proj, proj, x, wa, wb, wo, g2, wrh, wrl, br)


DMA_UNROLL = 8


def _tiles_to_rows(ref, slot, first, n, n_sub):
    return jnp.concatenate([ref[slot, pl.ds(first * n_sub + s, n, stride=n_sub), :] for s in range(n_sub)], axis=1)


def _gather_rows(src_hbm, idx_ref, idx0, dst, slot, sem, n, n_sub):
    def body(r, c):
        pltpu.make_async_copy(src_hbm.at[idx_ref[idx0 + r]],
                              dst.at[slot, pl.ds(pl.multiple_of(r * n_sub, n_sub), n_sub)], sem.at[slot]).start()
        return c
    lax.fori_loop(0, n, body, 0, unroll=DMA_UNROLL)


def _gather_wait(dst, slot, sem):
    pltpu.make_async_copy(dst.at[slot], dst.at[slot], sem.at[slot]).wait()


MOE_BM = 512
MOE_FC = 256
MOE_TC = 128
MOE_VMEM_LIMIT = 58 * 1024 * 1024
PAIR_W = 2 * LANES
CONV_ROWS = 256


def _moe_kernel(be_ref, bv_ref, bf_ref, br_ref, nxe_ref, nxok_ref, tok_ref,
                x_hbm, w1_hbm, w2_hbm, perm_ref, b1_ref, b2_ref, g_ref, o_ref,
                xbuf, xb_sc, acc_sc, w1b, w2b, s1, s2, gsem, wsem, *, bm, fc, n_chunks, n_sub):
    i = pl.program_id(0)
    nb = pl.num_programs(0)
    valid = bv_ref[i] > 0
    e = be_ref[i]
    d_model = xb_sc.shape[1]
    half = bm // 2

    def weight_copies(ex, c, slot):
        col = pl.multiple_of(c * (2 * fc), 2 * fc)
        row = pl.multiple_of(c * fc, fc)
        return (pltpu.make_async_copy(w1_hbm.at[ex, :, pl.ds(col, 2 * fc)], s1.at[slot], wsem.at[0, slot]),
                pltpu.make_async_copy(w2_hbm.at[ex, pl.ds(row, fc), :], s2.at[slot], wsem.at[1, slot]))

    def weights_start(ex, c, slot):
        for cp in weight_copies(ex, c, slot):
            cp.start()

    def weights_wait(ex, c, slot):
        for cp in weight_copies(ex, c, slot):
            cp.wait()

    @pl.when(i == 0)
    def _():
        _gather_rows(x_hbm, tok_ref, 0, xbuf, 0, gsem, bm, n_sub)
        weights_start(e, 0, 0)
        weights_start(e, 1, 1)

    def convert(c, slot):
        perm = perm_ref[...]
        for r in range(d_model // CONV_ROWS):
            rows = slice(r * CONV_ROWS, (r + 1) * CONV_ROWS)
            wt = s1[slot, rows, :].astype(BF16)
            for g in range(2 * fc // PAIR_W):
                cols = slice(g * PAIR_W, (g + 1) * PAIR_W)
                w1b[c, rows, cols] = _dot(wt[:, cols], perm).astype(BF16)
        w2b[c] = s2[slot].astype(BF16)

    def compute(c, n_rows):
        rows = slice(0, n_rows)
        col = pl.multiple_of(c * (2 * fc), 2 * fc)
        hcat = _dot(xb_sc[rows, :], w1b[c]) + b1_ref[0, :, pl.ds(col, 2 * fc)]
        acts = []
        for g in range(2 * fc // PAIR_W):
            glu = jnp.minimum(hcat[:, g * PAIR_W:g * PAIR_W + LANES], SWIGLU_LIMIT)
            lin = jnp.clip(hcat[:, g * PAIR_W + LANES:(g + 1) * PAIR_W], -SWIGLU_LIMIT, SWIGLU_LIMIT)
            acts.append(glu * jax.nn.sigmoid(SWIGLU_ALPHA * glu) * (lin + 1.0))
        acc_sc[rows, :] += _dot(jnp.concatenate(acts, axis=1).astype(BF16), w2b[c])

    def chunk_loop(first, n_rows):
        def body(c, carry):
            if first:
                slot = c % 2
                weights_wait(e, c, slot)
                convert(c, slot)

                @pl.when(c + 2 < n_chunks)
                def _():
                    weights_start(e, c + 2, slot)

                @pl.when((c + 2 >= n_chunks) & (nxok_ref[i] > 0))
                def _():
                    weights_start(nxe_ref[i], c + 2 - n_chunks, slot)
            compute(c, n_rows)
            return carry
        lax.fori_loop(0, n_chunks, body, 0)

    @pl.when(valid)
    def _():
        _gather_wait(xbuf, 0, gsem)
        for r0 in range(0, bm, half):
            xb_sc[r0:r0 + half, :] = _tiles_to_rows(xbuf, 0, r0, half, n_sub).astype(BF16)
        nxt = jnp.minimum(i + 1, nb - 1)

        @pl.when((i + 1 < nb) & (bv_ref[nxt] > 0))
        def _():
            _gather_rows(x_hbm, tok_ref, (i + 1) * bm, xbuf, 0, gsem, bm, n_sub)

        acc_sc[...] = jnp.broadcast_to(b2_ref[0], acc_sc.shape)
        is_first = bf_ref[i] > 0
        is_full = br_ref[i] > half
        for first in (True, False):
            for n_rows in (bm, half):
                cond_first = is_first if first else jnp.logical_not(is_first)
                cond_rows = is_full if n_rows == bm else jnp.logical_not(is_full)

                @pl.when(cond_first & cond_rows)
                def _():
                    chunk_loop(first, n_rows)

        for r0 in range(0, bm, half):
            y = acc_sc[r0:r0 + half, :] * g_ref[r0:r0 + half, :]
            for s in range(n_sub):
                o_ref[pl.ds(r0 * n_sub + s, half, stride=n_sub), :] = y[:, LANES * s:LANES * (s + 1)]

    @pl.when(jnp.logical_not(valid))
    def _():
        o_ref[...] = jnp.zeros(o_ref.shape, F32)


def _moe(route, x_tiles, w1, b1g, w2, b2, bm, fc):
    E, D, F2 = w1.shape
    n_chunks = F2 // (2 * fc)
    assert n_chunks % 2 == 0 and (2 * fc) % PAIR_W == 0 and D % CONV_ROWS == 0
    n_sub = x_tiles.shape[1]
    n_blocks = route['block_expert'].shape[0]
    col = jnp.arange(PAIR_W)
    src = jnp.where(col < LANES, 2 * col, 2 * (col - LANES) + 1)
    perm = (jnp.arange(PAIR_W)[:, None] == src[None, :]).astype(BF16)
    by_expert = lambda i, be, *_: (be[i], 0, 0)
    return pl.pallas_call(
        functools.partial(_moe_kernel, bm=bm, fc=fc, n_chunks=n_chunks, n_sub=n_sub),
        grid_spec=pltpu.PrefetchScalarGridSpec(
            num_scalar_prefetch=7,
            grid=(n_blocks,),
            in_specs=[
                pl.BlockSpec(memory_space=pl.ANY),
                pl.BlockSpec(memory_space=pl.ANY),
                pl.BlockSpec(memory_space=pl.ANY),
                pl.BlockSpec((PAIR_W, PAIR_W), lambda i, *_: (0, 0)),
                pl.BlockSpec((1, 1, F2), by_expert),
                pl.BlockSpec((1, 1, D), by_expert),
                pl.BlockSpec((bm, 1), lambda i, *_: (i, 0)),
            ],
            out_specs=pl.BlockSpec((bm * n_sub, LANES), lambda i, *_: (i, 0)),
            scratch_shapes=[
                pltpu.VMEM((1, bm * n_sub, LANES), F32), pltpu.VMEM((bm, D), BF16), pltpu.VMEM((bm, D), F32),
                pltpu.VMEM((n_chunks, D, 2 * fc), BF16), pltpu.VMEM((n_chunks, fc, D), BF16),
                pltpu.VMEM((2, D, 2 * fc), F32), pltpu.VMEM((2, fc, D), F32),
                pltpu.SemaphoreType.DMA((1,)), pltpu.SemaphoreType.DMA((2, 2)),
            ],
        ),
        out_shape=jax.ShapeDtypeStruct((n_blocks * bm * n_sub, LANES), F32),
        compiler_params=pltpu.CompilerParams(dimension_semantics=("arbitrary",), vmem_limit_bytes=MOE_VMEM_LIMIT),
        name="moe_experts",
    )(route['block_expert'], route['block_valid'], route['block_first'], route['block_rows'],
      route['next_expert'], route['next_ok'], route['row_tok'], x_tiles, w1, w2, perm, b1g, b2,
      route['row_gate']).reshape(n_blocks * bm, n_sub, LANES)


def _combine_kernel(pos_ref, y_hbm, h_ref, o_ref, buf, sem, *, tm, top_k, n_sub):
    i = pl.program_id(0)
    slot = i % 2
    n = top_k * tm

    @pl.when(i == 0)
    def _():
        _gather_rows(y_hbm, pos_ref, 0, buf, 0, sem, n, n_sub)

    @pl.when(i + 1 < pl.num_programs(0))
    def _():
        _gather_rows(y_hbm, pos_ref, (i + 1) * n, buf, 1 - slot, sem, n, n_sub)

    _gather_wait(buf, slot, sem)
    acc = h_ref[...]
    for k in range(top_k):
        acc = acc + _tiles_to_rows(buf, slot, k * tm, tm, n_sub)
    o_ref[...] = acc


def _combine(pos, y_tiles, h, tm):
    T, D = h.shape
    n_sub = y_tiles.shape[1]
    return pl.pallas_call(
        functools.partial(_combine_kernel, tm=tm, top_k=TOP_K, n_sub=n_sub),
        grid_spec=pltpu.PrefetchScalarGridSpec(
            num_scalar_prefetch=1,
            grid=(T // tm,),
            in_specs=[
                pl.BlockSpec(memory_space=pl.ANY),
                pl.BlockSpec((tm, D), lambda i, *_: (i, 0)),
            ],
            out_specs=pl.BlockSpec((tm, D), lambda i, *_: (i, 0)),
            scratch_shapes=[pltpu.VMEM((2, TOP_K * tm * n_sub, LANES), F32), pltpu.SemaphoreType.DMA((2,))],
        ),
        out_shape=jax.ShapeDtypeStruct((T, D), F32),
        compiler_params=_cparams(("arbitrary",)),
        name="moe_combine",
    )(pos, y_tiles, h)


def _routing(top_idx, top_gate, n_experts, bm, tc):
    T = top_idx.shape[1]
    N = TOP_K * T
    i32 = jnp.int32
    e_flat = top_idx.reshape(-1)
    iota = jnp.arange(N, dtype=i32)
    experts = jnp.arange(n_experts, dtype=i32)
    e_sorted, order = lax.sort_key_val(e_flat, iota)
    counts = jnp.sum((e_flat[:, None] == experts[None, :]).astype(i32), axis=0)
    starts = jnp.cumsum(counts) - counts
    padded = ((counts + bm - 1) // bm) * bm
    pad_ends = jnp.cumsum(padded)
    pad_starts = pad_ends - padded
    dest_sorted = pad_starts[e_sorted] + (iota - starts[e_sorted])
    _, dest = lax.sort_key_val(order, dest_sorted)
    pos = dest.reshape(TOP_K, T // tc, tc).transpose(1, 0, 2).reshape(-1)

    n_blocks = -(-N // bm) + n_experts
    blk_start = jnp.arange(n_blocks, dtype=i32) * bm
    blk_e = jnp.minimum(jnp.sum((pad_ends[None, :] <= blk_start[:, None]).astype(i32), axis=1), n_experts - 1)
    block_valid = (blk_start < pad_ends[-1]).astype(i32)
    block_rows = block_valid * jnp.clip(counts[blk_e] - (blk_start - pad_starts[blk_e]), 0, bm)
    prev_e = jnp.concatenate([jnp.full((1,), -1, i32), blk_e[:-1]])
    block_first = block_valid * (blk_e != prev_e).astype(i32)
    last_e = jnp.max(jnp.where(counts > 0, experts, 0))
    block_expert = jnp.where(block_valid > 0, blk_e, last_e)
    later = lax.cummin(jnp.where(counts > 0, experts, n_experts)[::-1])[::-1]
    nxt = jnp.concatenate([later[1:], jnp.full((1,), n_experts, i32)])[block_expert]
    next_ok = (nxt < n_experts).astype(i32)
    next_expert = jnp.minimum(nxt, n_experts - 1)

    row = jnp.arange(n_blocks * bm, dtype=i32)
    row_e = jnp.repeat(blk_e, bm)
    within = row - pad_starts[row_e]
    row_ok = (within < counts[row_e]) & (row < pad_ends[-1])
    src = jnp.clip(starts[row_e] + within, 0, N - 1)
    row_tok = jnp.where(row_ok, (order % T)[src], 0)
    row_gate = jnp.where(row_ok, top_gate.reshape(-1)[order][src], 0.0)
    return dict(row_tok=row_tok, row_gate=row_gate.reshape(-1, 1), pos=pos, block_expert=block_expert,
                block_valid=block_valid, block_first=block_first, block_rows=block_rows,
                next_expert=next_expert, next_ok=next_ok)


def _tile(n, pref):
    for t in pref:
        if n % t == 0:
            return t
    raise ValueError(f"no tile for {n}")


def _layer(layer, xp, xs, cache_k, cache_v, state_conv, page_table, p):
    n_b, seq, D = xp.shape
    dec_b, dec_seq, _ = xs.shape
    assert n_b == 1
    _, n_phys, page, n_heads, _, head_dim = cache_k.shape
    attn_w = n_heads * 2 * head_dim
    conv_w = p['w_conv'].shape[1]
    assert conv_w == attn_w
    n_exp = p['w_router'].shape[1]
    d_ff = p['w_mlp2'].shape[1]
    Tp, Ts = n_b * seq, dec_b * dec_seq
    T = Tp + Ts
    lam0 = 0.8 - 0.6 * math.exp(-0.3 * layer)

    x = jnp.concatenate([xp.reshape(Tp, D), xs.reshape(Ts, D)], axis=0)

    tm = _tile(math.gcd(Tp, Ts), (512, 256, 128))
    proj = _in_proj(x, p['norm1_g'], p['w_in'].astype(BF16), tm, 1024)

    tm2 = _tile(math.gcd(Tp, Ts), (256, 128))
    t_idx = jnp.arange(T, dtype=jnp.int32)
    pos_in_seq = jnp.where(t_idx < Tp, t_idx, (t_idx - Tp) % dec_seq)
    m1 = (pos_in_seq >= 1).astype(F32).reshape(T, 1)
    m2 = (pos_in_seq >= 2).astype(F32).reshape(T, 1)
    st = state_conv.astype(F32)
    zero = jnp.zeros((dec_b, dec_seq - 1, conv_w), F32)
    s1 = jnp.concatenate([st[:, 1:2], zero], axis=1).reshape(Ts, conv_w)
    s2 = jnp.concatenate([st[:, 0:2], zero[:, 1:]], axis=1).reshape(Ts, conv_w)
    reps = attn_w // head_dim
    gq = jnp.tile(p['q_norm_g'].astype(F32), reps).reshape(1, attn_w)
    gk = jnp.tile(p['k_norm_g'].astype(F32), reps).reshape(1, attn_w)
    lane = jnp.arange(LANES)
    gmat = ((lane[:, None] // head_dim == lane[None, :] // head_dim).astype(F32) / head_dim).astype(BF16)
    f32 = F32
    slopes = 2.0 ** (-8.0 * jnp.arange(1, n_heads + 1, dtype=f32) / n_heads)
    a_log2 = slopes * LOG2E
    a_hi, a_lo = _split_bf16(a_log2)
    off = jnp.arange(attn_w) % (2 * head_dim)
    a_hi_l, a_lo_l = jnp.repeat(a_hi.astype(f32), 2 * head_dim), jnp.repeat(a_lo.astype(f32), 2 * head_dim)
    aug_row = lambda o: jnp.where(o < 0, 0.0, jnp.where(o < 2, a_hi_l, jnp.where(o < 4, a_lo_l, 0.0)))
    qaug = jnp.stack([aug_row(off - head_dim), aug_row(off)]).reshape(2, 1, attn_w)
    blk = _tile(Tp, (512, 256, 128))
    ya, u, qa, ka, qs, kf, vb = _mix_prep(proj, m1, m2, s1, s2, p['w_conv'].astype(F32), gq, gk, gmat, qaug,
                                         tm2, Tp // tm2, head_dim, blk)
    vf = proj[:, 3 * conv_w + 2 * attn_w:3 * conv_w + 3 * attn_w]

    lam = (jnp.exp(jnp.sum(p['lambda_q1'].astype(f32) * p['lambda_k1'].astype(f32)))
           - jnp.exp(jnp.sum(p['lambda_q2'].astype(f32) * p['lambda_k2'].astype(f32))) + lam0).reshape(1)
    g_sub = p['subln_g'].astype(F32).reshape(1, 2 * head_dim)
    yb_p = _prompt_attn(a_hi.astype(f32) + a_lo.astype(f32), lam, qa, ka, vb, g_sub, Tp, n_heads, blk, lam0)

    pad_rows = lambda a: jnp.pad(a.reshape(dec_b, dec_seq, attn_w), ((0, 0), (0, QPAD - dec_seq), (0, 0)))
    q_s = pad_rows(qs[Tp:])
    yb_s = _paged_attn(page_table, lam, q_s,
                       jnp.transpose(cache_k, (0, 1, 3, 4, 5, 2)), cache_v, layer,
                       pad_rows(kf[Tp:]), pad_rows(vf[Tp:]), g_sub, dec_seq, n_heads, head_dim, lam0)
    yb = jnp.concatenate([yb_p, yb_s[:, :dec_seq].reshape(Ts, attn_w)], axis=0)

    wr = p['w_router'].astype(F32).T
    wrh, wrl = _split_bf16(wr)
    h, xn2, top_idx, top_gate = _out_proj(
        ya, yb, proj, x, p['w_proj_a'].astype(BF16), p['w_proj_b'].astype(BF16), p['w_o'].astype(BF16),
        p['norm2_g'].astype(F32).reshape(1, D), wrh, wrl, p['b_router'].astype(F32).reshape(n_exp, 1),
        tm2, (3 * conv_w + 3 * attn_w) // D)

    bm, fc, tc = MOE_BM, MOE_FC, MOE_TC
    route = _routing(top_idx, top_gate, n_exp, bm, tc)
    b1g = (p['b_mlp1'].astype(F32).reshape(n_exp, 2 * d_ff // PAIR_W, LANES, 2).transpose(0, 1, 3, 2)
           .reshape(n_exp, 1, 2 * d_ff))
    yr = _moe(route, xn2.reshape(T, D // LANES, LANES), p['w_mlp1'].astype(F32), b1g,
              p['w_mlp2'].astype(F32), p['b_mlp2'].astype(F32).reshape(n_exp, 1, D), bm, fc)
    y = _combine(route['pos'], yr, h, tc)

    k_p = kf[:Tp].reshape(n_b, seq, n_heads, 2, head_dim)
    v_p = vf[:Tp].reshape(n_b, seq, n_heads, 2 * head_dim)
    c_p = u[Tp - (CONV_K - 1):Tp].reshape(n_b, CONV_K - 1, conv_w)
    k_s = kf[Tp:].reshape(dec_b, dec_seq, n_heads, 2, head_dim)
    v_s = vf[Tp:].reshape(dec_b, dec_seq, n_heads, 2 * head_dim)
    c_s = u[Tp:].reshape(dec_b, dec_seq, conv_w)[:, dec_seq - (CONV_K - 1):]
    return (y[:Tp].reshape(n_b, seq, D), y[Tp:].reshape(dec_b, dec_seq, D), k_p, v_p, c_p, k_s, v_s, c_s)


def kernel(x_prompt, x_sample, cache_k, cache_v, state_conv, page_table, norm1_g, w_in, w_conv, q_norm_g, k_norm_g, lambda_q1, lambda_k1, lambda_q2, lambda_k2, subln_g, w_proj_a, w_proj_b, w_o, norm2_g, w_router, b_router, w_mlp1, b_mlp1, w_mlp2, b_mlp2):
    depth = w_in.shape[0]
    names = ('norm1_g', 'w_in', 'w_conv', 'q_norm_g', 'k_norm_g', 'lambda_q1', 'lambda_k1', 'lambda_q2',
             'lambda_k2', 'subln_g', 'w_proj_a', 'w_proj_b', 'w_o', 'norm2_g', 'w_router', 'b_router',
             'w_mlp1', 'b_mlp1', 'w_mlp2', 'b_mlp2')
    stacked = (norm1_g, w_in, w_conv, q_norm_g, k_norm_g, lambda_q1, lambda_k1, lambda_q2, lambda_k2,
               subln_g, w_proj_a, w_proj_b, w_o, norm2_g, w_router, b_router, w_mlp1, b_mlp1, w_mlp2, b_mlp2)
    h_p, h_s = x_prompt, x_sample
    outs = [[] for _ in range(6)]
    for l in range(depth):
        p = {n: a[l] for n, a in zip(names, stacked)}
        res = _layer(l, h_p, h_s, cache_k, cache_v, state_conv[l], page_table, p)
        h_p, h_s = res[0], res[1]
        for acc, r in zip(outs, res[2:]):
            acc.append(r)
    return (h_p, h_s) + tuple(jnp.stack(o) for o in outs)
```

```python
import functools
import math

import jax
import jax.numpy as jnp
from jax import lax
from jax.experimental import pallas as pl
from jax.experimental.pallas import tpu as pltpu

F32 = jnp.float32
BF16 = jnp.bfloat16

TOP_K = 4
CONV_K = 3
NORM_EPS = 1e-6
NEG_INF = -1e30
SWIGLU_ALPHA = 1.702
SWIGLU_LIMIT = 7.0
LANES = 128
VMEM_LIMIT = 56 * 1024 * 1024


def _cparams(sem, flags=None):
    return pltpu.CompilerParams(dimension_semantics=sem, vmem_limit_bytes=VMEM_LIMIT, flags=flags)


def _nt_dot(a, b):
    return lax.dot_general(a, b, (((1,), (1,)), ((), ())), preferred_element_type=F32)


def _dot(a, b):
    return jnp.dot(a, b, preferred_element_type=F32)


def _split_bf16(x):
    hi = x.astype(BF16)
    lo = (x - hi.astype(F32)).astype(BF16)
    return hi, lo


def _in_proj_kernel(x_ref, g_ref, w_ref, o_ref, xn_sc):
    @pl.when(pl.program_id(1) == 0)
    def _():
        x = x_ref[...]
        ms = jnp.mean(x * x, axis=-1, keepdims=True)
        xn_sc[...] = (x * lax.rsqrt(ms + NORM_EPS) * g_ref[...]).astype(BF16)

    o_ref[...] = _dot(xn_sc[...], w_ref[...])


def _in_proj(x, g, w_bf16, tm, tn):
    T, D = x.shape
    N = w_bf16.shape[1]
    return pl.pallas_call(
        _in_proj_kernel,
        grid=(T // tm, N // tn),
        in_specs=[
            pl.BlockSpec((tm, D), lambda i, j: (i, 0)),
            pl.BlockSpec((1, D), lambda i, j: (0, 0)),
            pl.BlockSpec((D, tn), lambda i, j: (0, j)),
        ],
        out_specs=pl.BlockSpec((tm, tn), lambda i, j: (i, j)),
        out_shape=jax.ShapeDtypeStruct((T, N), F32),
        scratch_shapes=[pltpu.VMEM((tm, D), BF16)],
        compiler_params=_cparams(("parallel", "arbitrary")),
        name="in_proj",
    )(x, g.reshape(1, D), w_bf16)


def _group_mean_sq(x, gmat):
    hi, lo = _split_bf16(x * x)
    outs = []
    for j in range(x.shape[1] // LANES):
        sl = slice(LANES * j, LANES * (j + 1))
        outs.append(_dot(hi[:, sl], gmat) + _dot(lo[:, sl], gmat))
    return jnp.concatenate(outs, axis=1)


LOG2E = math.log2(math.e)
KPOS_SPLIT = 256


def _mix_prep_kernel(cb_ref, cc_ref, cx_ref, ccp_ref, cxp_ref, q_ref, k_ref, v_ref,
                     m1_ref, m2_ref, s1_ref, s2_ref, wc_ref, gq_ref, gk_ref, gmat_ref, qaug_ref,
                     ya_ref, u_ref, qa_ref, ka_ref, qs_ref, kf_ref, vb_ref,
                     *, n_prompt_tiles, head_dim, kv_block):
    i = pl.program_id(0)
    u = cc_ref[...] * cx_ref[...]
    u_ref[...] = u
    up = ccp_ref[...] * cxp_ref[...]
    row = lax.broadcasted_iota(jnp.int32, u.shape, 0)
    p1 = pltpu.roll(u, 1, 0)
    p1 = jnp.where(row == 0, up[7:8, :], p1)
    p2 = pltpu.roll(u, 2, 0)
    p2 = jnp.where(row == 0, up[6:7, :], jnp.where(row == 1, up[7:8, :], p2))
    is_sample = i >= n_prompt_tiles
    s1 = jnp.where(is_sample, s1_ref[...], 0.0)
    s2 = jnp.where(is_sample, s2_ref[...], 0.0)
    p1 = p1 * m1_ref[...] + s1
    p2 = p2 * m2_ref[...] + s2
    w = wc_ref[...]
    conv = p2 * w[0:1, :] + p1 * w[1:2, :] + u * w[2:3, :]
    ya_ref[...] = (cb_ref[...] * conv).astype(BF16)

    gmat = gmat_ref[...]
    q = q_ref[...]
    qn = q * lax.rsqrt(_group_mean_sq(q, gmat) + NORM_EPS) * gq_ref[...] * (head_dim ** -0.5)
    qs_ref[...] = qn
    k = k_ref[...]
    kn = k * lax.rsqrt(_group_mean_sq(k, gmat) + NORM_EPS) * gk_ref[...]
    kf_ref[...] = kn
    vb_ref[...] = v_ref[...].astype(BF16)

    lane = lax.broadcasted_iota(jnp.int32, q.shape, 1)
    off = lane % (2 * head_dim)
    first_map = off < head_dim
    pos = (i * q.shape[0] + lax.broadcasted_iota(jnp.int32, (q.shape[0], 1), 0)) % kv_block
    pos_lo = (pos % KPOS_SPLIT).astype(F32)
    pos_hi = (pos - pos % KPOS_SPLIT).astype(F32)
    ql = qn * LOG2E
    for c in range(2):
        data = first_map if c == 0 else jnp.logical_not(first_map)
        aug = off - (1 - c) * head_dim
        k_aug = jnp.where((aug == 0) | (aug == 2), pos_lo, jnp.where((aug == 1) | (aug == 3), pos_hi, 0.0))
        qa_ref[c] = jnp.where(data, ql, qaug_ref[c]).astype(BF16)
        ka_ref[c] = jnp.where(data, kn, k_aug).astype(BF16)


def _mix_prep(proj, m1, m2, s1, s2, w_conv, gq, gk, gmat, qaug, tm, n_prompt_tiles, head_dim, kv_block):
    T = proj.shape[0]
    W = w_conv.shape[1]
    col = lambda c: pl.BlockSpec((tm, W), lambda i, c=c: (i, c))
    prev = lambda c: pl.BlockSpec((8, W), lambda i, c=c: (jnp.maximum(i * (tm // 8) - 1, 0), c))
    samp = pl.BlockSpec((tm, W), lambda i: (jnp.maximum(i - n_prompt_tiles, 0), 0))
    rowv = pl.BlockSpec((tm, 1), lambda i: (i, 0))
    full = lambda a: pl.BlockSpec(a.shape, lambda i: (0,) * a.ndim)
    out_row = pl.BlockSpec((tm, W), lambda i: (i, 0))
    both_maps = pl.BlockSpec((2, tm, W), lambda i: (0, i, 0))
    return pl.pallas_call(
        functools.partial(_mix_prep_kernel, n_prompt_tiles=n_prompt_tiles, head_dim=head_dim, kv_block=kv_block),
        grid=(T // tm,),
        in_specs=[col(0), col(1), col(2), prev(1), prev(2), col(3), col(4), col(5),
                  rowv, rowv, samp, samp, full(w_conv), full(gq), full(gk), full(gmat), full(qaug)],
        out_specs=[out_row, out_row, both_maps, both_maps, out_row, out_row, out_row],
        out_shape=[jax.ShapeDtypeStruct((T, W), BF16), jax.ShapeDtypeStruct((T, W), F32),
                   jax.ShapeDtypeStruct((2, T, W), BF16), jax.ShapeDtypeStruct((2, T, W), BF16),
                   jax.ShapeDtypeStruct((T, W), F32), jax.ShapeDtypeStruct((T, W), F32),
                   jax.ShapeDtypeStruct((T, W), BF16)],
        compiler_params=_cparams(("parallel",)),
        name="mix_prep",
    )(proj, proj, proj, proj, proj, proj, proj, proj, m1, m2, s1, s2, w_conv, gq, gk, gmat, qaug)


def _sub_norm(o, g, lam0):
    ms = jnp.mean(o * o, axis=-1, keepdims=True)
    return o * lax.rsqrt(ms + NORM_EPS) * g * (1.0 - lam0)


ATTN_SUB = 128


def _prompt_attn_kernel(a_ref, lam_ref, ii_ref, jj_ref, qa_ref, ka_ref, v_ref, g_ref, o_ref,
                        m_sc, acc_sc, *, blk, sub, lam0):
    h, t = pl.program_id(0), pl.program_id(1)
    i, j = ii_ref[t], jj_ref[t]
    hd2 = v_ref.shape[1]

    @pl.when(j == 0)
    def _():
        m_sc[...] = jnp.full(m_sc.shape, NEG_INF, F32)
        acc_sc[...] = jnp.zeros(acc_sc.shape, F32)

    v_ones = jnp.concatenate([v_ref[...], jnp.ones((blk, hd2), BF16)], axis=1)

    def update(masked):
        for r0 in range(0, blk, sub):
            rows = pl.ds(r0, sub)
            row = r0 + lax.broadcasted_iota(jnp.int32, (sub, 1), 0)
            row_term = a_ref[h] * ((i - j) * blk + row).astype(F32)
            for c in range(2):
                s = _nt_dot(qa_ref[c, rows, :], ka_ref[c])
                if masked:
                    s = jnp.where(lax.broadcasted_iota(jnp.int32, s.shape, 1) <= row, s, NEG_INF)
                m_prev = m_sc[c, rows, :]
                m_new = jnp.maximum(m_prev, jnp.max(s, axis=1, keepdims=True) - row_term)
                p = jnp.exp2(s - (row_term + m_new)).astype(BF16)
                acc_sc[c, rows, :] = jnp.exp2(m_prev - m_new) * acc_sc[c, rows, :] + _dot(p, v_ones)
                m_sc[c, rows, :] = m_new

    @pl.when(j < i)
    def _():
        update(False)

    @pl.when(j == i)
    def _():
        update(True)
        a0, a1 = acc_sc[0], acc_sc[1]
        o = a0[:, :hd2] / a0[:, hd2:] - lam_ref[0] * (a1[:, :hd2] / a1[:, hd2:])
        o_ref[...] = _sub_norm(o, g_ref[...], lam0).astype(BF16)


def _prompt_attn(a_log2, lam, qa, ka, vb, g_sub, n_rows, n_heads, blk, lam0):
    hd2 = g_sub.shape[1]
    nb = n_rows // blk
    pairs = [(i, j) for i in range(nb) for j in range(i + 1)]
    ii = jnp.asarray([p[0] for p in pairs], jnp.int32)
    jj = jnp.asarray([p[1] for p in pairs], jnp.int32)
    return pl.pallas_call(
        functools.partial(_prompt_attn_kernel, blk=blk, sub=min(blk, ATTN_SUB), lam0=lam0),
        grid_spec=pltpu.PrefetchScalarGridSpec(
            num_scalar_prefetch=4,
            grid=(n_heads, len(pairs)),
            in_specs=[
                pl.BlockSpec((2, blk, hd2), lambda h, t, a, l, ii, jj: (0, ii[t], h)),
                pl.BlockSpec((2, blk, hd2), lambda h, t, a, l, ii, jj: (0, jj[t], h)),
                pl.BlockSpec((blk, hd2), lambda h, t, a, l, ii, jj: (jj[t], h)),
                pl.BlockSpec((1, hd2), lambda h, t, *_: (0, 0)),
            ],
            out_specs=pl.BlockSpec((blk, hd2), lambda h, t, a, l, ii, jj: (ii[t], h)),
            scratch_shapes=[pltpu.VMEM((2, blk, 1), F32), pltpu.VMEM((2, blk, 2 * hd2), F32)],
        ),
        out_shape=jax.ShapeDtypeStruct((n_rows, n_heads * hd2), BF16),
        compiler_params=_cparams(("parallel", "arbitrary")),
        name="prompt_attn",
    )(a_log2, lam, ii, jj, qa, ka, vb, g_sub)


QPAD = 8


def _paged_attn_kernel(pt_ref, lam_ref, q_ref, *rest, n_pages, pps, page, n_new, n_heads, head_dim, lam0):
    kc_refs, vc_refs = rest[:pps], rest[pps:2 * pps]
    kn_ref, vn_ref, g_ref, o_ref, qbd_sc, m_sc, l_sc, acc_sc = rest[2 * pps:]
    p = pl.program_id(1)
    n_steps = n_pages // pps
    n_maps = 2 * n_heads
    rows = n_maps * QPAD
    width = n_maps * head_dim
    r = lax.broadcasted_iota(jnp.int32, (rows, 1), 0)
    qi = r % QPAD
    slope = jnp.exp2(-(r // (2 * QPAD) + 1).astype(F32))

    @pl.when(p == 0)
    def _():
        q = jnp.tile(q_ref[0], (n_maps, 1))
        rr = lax.broadcasted_iota(jnp.int32, (rows, width), 0)
        ll = lax.broadcasted_iota(jnp.int32, (rows, width), 1)
        qbd_sc[...] = jnp.where(ll // head_dim == rr // QPAD, q, 0.0).astype(BF16)
        m_sc[...] = jnp.full(m_sc.shape, NEG_INF, F32)
        l_sc[...] = jnp.zeros(l_sc.shape, F32)
        acc_sc[...] = jnp.zeros(acc_sc.shape, F32)

    def step(s, vblk):
        m_prev = m_sc[...]
        m_new = jnp.maximum(m_prev, jnp.max(s, axis=1, keepdims=True))
        pr = jnp.exp(s - m_new)
        alpha = jnp.exp(m_prev - m_new)
        l_sc[...] = alpha * l_sc[...] + jnp.sum(pr, axis=1, keepdims=True)
        acc_sc[...] = alpha * acc_sc[...] + _dot(pr.astype(BF16), vblk)
        m_sc[...] = m_new

    past = n_pages * page

    @pl.when(p < n_steps)
    def _():
        key = lax.broadcasted_iota(jnp.int32, (rows, pps * page), 1)
        dist = (past + qi - (p * (pps * page) + key)).astype(F32)
        kt = jnp.concatenate([kc[0, 0].reshape(width, page).astype(BF16) for kc in kc_refs], axis=1)
        vblk = jnp.concatenate(
            [jnp.concatenate([vc[0, 0, pl.ds(hh, page, stride=n_heads), :] for hh in range(n_heads)],
                             axis=1).astype(BF16) for vc in vc_refs], axis=0)
        step(_dot(qbd_sc[...], kt) - slope * dist, vblk)

    @pl.when(p == n_steps)
    def _():
        key = lax.broadcasted_iota(jnp.int32, (rows, page), 1)
        zeros = jnp.zeros((page - QPAD, width), F32)
        kblk = jnp.concatenate([kn_ref[0], zeros], axis=0).astype(BF16)
        vblk = jnp.concatenate([vn_ref[0], zeros], axis=0).astype(BF16)
        dist = (qi - key).astype(F32)
        s = _nt_dot(qbd_sc[...], kblk) - slope * dist
        step(jnp.where((key <= qi) & (key < n_new), s, NEG_INF), vblk)

        c = r // QPAD
        wgt = jnp.where(c % 2 == 0, 1.0, -lam_ref[0]) / l_sc[...]
        rr = lax.broadcasted_iota(jnp.int32, (rows, width), 0)
        ll = lax.broadcasted_iota(jnp.int32, (rows, width), 1)
        own = ll // (2 * head_dim) == rr // (2 * QPAD)
        accw = jnp.where(own, acc_sc[...] * wgt, 0.0)
        o = jnp.sum(accw.reshape(n_maps, QPAD, width), axis=0)
        g = g_ref[...]
        hd2 = 2 * head_dim
        o_ref[0] = jnp.concatenate(
            [_sub_norm(o[:, hd2 * hh:hd2 * (hh + 1)], g, lam0) for hh in range(n_heads)],
            axis=1).astype(BF16)


def _paged_attn(page_table, lam, q_pad, cache_kt, cache_v, layer, k_new, v_new, g_sub,
                n_new, n_heads, head_dim, lam0):
    B, n_pages = page_table.shape
    page = cache_v.shape[2]
    width = 2 * n_heads * head_dim
    rows = 2 * n_heads * QPAD
    pps = next(c for c in (8, 4, 2, 1) if n_pages % c == 0)
    n_steps = n_pages // pps

    def page_of(b, p, pt, s):
        return pt[b, jnp.minimum(p, n_steps - 1) * pps + s]

    k_specs = [pl.BlockSpec((1, 1, n_heads, 2, head_dim, page),
                            lambda b, p, pt, *_, s=s: (layer, page_of(b, p, pt, s), 0, 0, 0, 0)) for s in range(pps)]
    v_rows = cache_v.reshape(cache_v.shape[0], cache_v.shape[1], page * n_heads, 2 * head_dim)
    v_specs = [pl.BlockSpec((1, 1, page * n_heads, 2 * head_dim),
                            lambda b, p, pt, *_, s=s: (layer, page_of(b, p, pt, s), 0, 0)) for s in range(pps)]
    seq_blk = pl.BlockSpec((1, QPAD, width), lambda b, p, *_: (b, 0, 0))
    return pl.pallas_call(
        functools.partial(_paged_attn_kernel, n_pages=n_pages, pps=pps, page=page, n_new=n_new,
                          n_heads=n_heads, head_dim=head_dim, lam0=lam0),
        grid_spec=pltpu.PrefetchScalarGridSpec(
            num_scalar_prefetch=2,
            grid=(B, n_steps + 1),
            in_specs=[seq_blk] + k_specs + v_specs + [
                seq_blk, seq_blk,
                pl.BlockSpec((1, 2 * head_dim), lambda b, p, *_: (0, 0)),
            ],
            out_specs=seq_blk,
            scratch_shapes=[pltpu.VMEM((rows, width), BF16), pltpu.VMEM((rows, 1), F32),
                            pltpu.VMEM((rows, 1), F32), pltpu.VMEM((rows, width), F32)],
        ),
        out_shape=jax.ShapeDtypeStruct((B, QPAD, width), BF16),
        compiler_params=_cparams(("parallel", "arbitrary")),
        name="paged_attn",
    )(page_table, lam, q_pad, *([cache_kt] * pps), *([v_rows] * pps), k_new, v_new, g_sub)


def _out_proj_kernel(ya_ref, yb_ref, ga_ref, gb_ref, x_ref, wa_ref, wb_ref, wo_ref, g2_ref,
                     wrh_ref, wrl_ref, br_ref, h_ref, xn_ref, ti_ref, tg_ref, cnt_ref, *, top_k):
    pa = _dot(ya_ref[...], wa_ref[...])
    pb = _dot(yb_ref[...], wb_ref[...])
    merged = jax.nn.sigmoid(ga_ref[...]) * pa + jax.nn.sigmoid(gb_ref[...]) * pb
    h = x_ref[...] + _dot(merged.astype(BF16), wo_ref[...])
    h_ref[...] = h
    ms = jnp.mean(h * h, axis=-1, keepdims=True)
    xn = h * lax.rsqrt(ms + NORM_EPS) * g2_ref[...]
    xn_ref[...] = xn

    xh, xl = _split_bf16(xn)
    wrh = wrh_ref[...]
    lg = _nt_dot(wrh, xh) + _nt_dot(wrh, xl) + _nt_dot(wrl_ref[...], xh) + br_ref[...]
    n_exp = lg.shape[0]
    row = lax.broadcasted_iota(jnp.int32, lg.shape, 0)
    vals, idxs = [], []
    chosen = jnp.zeros(lg.shape, F32)
    for _ in range(top_k):
        mx = jnp.max(lg, axis=0, keepdims=True)
        idx = jnp.min(jnp.where(lg == mx, row, n_exp), axis=0, keepdims=True)
        vals.append(mx)
        idxs.append(idx)
        chosen = chosen + (row == idx).astype(F32)
        lg = jnp.where(row == idx, -jnp.inf, lg)
    es = [jnp.exp(v - vals[0]) for v in vals]
    den = es[0]
    for e in es[1:]:
        den = den + e
    ti_ref[...] = jnp.concatenate(idxs, axis=0)
    tg_ref[...] = jnp.concatenate([e / den for e in es], axis=0)

    @pl.when(pl.program_id(0) == 0)
    def _():
        cnt_ref[...] = jnp.zeros(cnt_ref.shape, F32)

    cnt_ref[...] += jnp.sum(chosen, axis=1, keepdims=True)


def _out_proj(ya, yb, proj, x, wa, wb, wo, g2, wrh, wrl, br, tm, gate_col):
    T, D = x.shape
    W = ya.shape[1]
    E = wrh.shape[0]
    const = lambda a: pl.BlockSpec(a.shape, lambda i: (0,) * a.ndim, pipeline_mode=pl.Buffered(1))
    row = lambda w: pl.BlockSpec((tm, w), lambda i: (i, 0))
    return pl.pallas_call(
        functools.partial(_out_proj_kernel, top_k=TOP_K),
        grid=(T // tm,),
        in_specs=[row(W), row(W),
                  pl.BlockSpec((tm, D), lambda i: (i, gate_col)),
                  pl.BlockSpec((tm, D), lambda i: (i, gate_col + 1)),
                  row(D), const(wa), const(wb), const(wo), const(g2),
                  const(wrh), const(wrl), const(br)],
        out_specs=[row(D), row(D),
                   pl.BlockSpec((TOP_K, tm), lambda i: (0, i)),
                   pl.BlockSpec((TOP_K, tm), lambda i: (0, i)),
                   pl.BlockSpec((E, 1), lambda i: (0, 0))],
        out_shape=[jax.ShapeDtypeStruct((T, D), F32), jax.ShapeDtypeStruct((T, D), F32),
                   jax.ShapeDtypeStruct((TOP_K, T), jnp.int32),
                   jax.ShapeDtypeStruct((TOP_K, T), F32),
                   jax.ShapeDtypeStruct((E, 1), F32)],
        compiler_params=_cparams(("arbitrary",)),
        name="out_proj_router",
    )(ya, yb, proj, proj, x, wa, wb, wo, g2, wrh, wrl, br)


DMA_UNROLL = 8


def _tiles_to_rows(ref, slot, first, n, n_sub):
    return jnp.concatenate([ref[slot, pl.ds(first * n_sub + s, n, stride=n_sub), :] for s in range(n_sub)], axis=1)


def _gather_rows(src_hbm, idx_ref, idx0, dst, slot, sem, n, n_sub):
    def body(r, c):
        pltpu.make_async_copy(src_hbm.at[idx_ref[idx0 + r]],
                              dst.at[slot, pl.ds(pl.multiple_of(r * n_sub, n_sub), n_sub)], sem.at[slot]).start()
        return c
    lax.fori_loop(0, n, body, 0, unroll=DMA_UNROLL)


def _gather_wait(dst, slot, sem):
    pltpu.make_async_copy(dst.at[slot], dst.at[slot], sem.at[slot]).wait()


MOE_BM = 512
MOE_FC = 256
MOE_TC = 128
MOE_VMEM_LIMIT = 58 * 1024 * 1024
PAIR_W = 2 * LANES
CONV_ROWS = 256


def _moe_kernel(be_ref, bv_ref, bf_ref, br_ref, nxe_ref, nxok_ref, tok_ref,
                x_hbm, w1_hbm, w2_hbm, perm_ref, b1_ref, b2_ref, g_ref, o_ref,
                xbuf, xb_sc, acc_sc, w1b, w2b, s1, s2, gsem, wsem, *, bm, fc, n_chunks, n_sub):
    i = pl.program_id(0)
    nb = pl.num_programs(0)
    valid = bv_ref[i] > 0
    e = be_ref[i]
    d_model = xb_sc.shape[1]
    half = bm // 2

    def weight_copies(ex, c, slot):
        col = pl.multiple_of(c * (2 * fc), 2 * fc)
        row = pl.multiple_of(c * fc, fc)
        return (pltpu.make_async_copy(w1_hbm.at[ex, :, pl.ds(col, 2 * fc)], s1.at[slot], wsem.at[0, slot]),
                pltpu.make_async_copy(w2_hbm.at[ex, pl.ds(row, fc), :], s2.at[slot], wsem.at[1, slot]))

    def weights_start(ex, c, slot):
        for cp in weight_copies(ex, c, slot):
            cp.start()

    def weights_wait(ex, c, slot):
        for cp in weight_copies(ex, c, slot):
            cp.wait()

    @pl.when(i == 0)
    def _():
        _gather_rows(x_hbm, tok_ref, 0, xbuf, 0, gsem, bm, n_sub)
        weights_start(e, 0, 0)
        weights_start(e, 1, 1)

    def convert(c, slot):
        perm = perm_ref[...]
        for r in range(d_model // CONV_ROWS):
            rows = slice(r * CONV_ROWS, (r + 1) * CONV_ROWS)
            wt = s1[slot, rows, :].astype(BF16)
            for g in range(2 * fc // PAIR_W):
                cols = slice(g * PAIR_W, (g + 1) * PAIR_W)
                w1b[c, rows, cols] = _dot(wt[:, cols], perm).astype(BF16)
        w2b[c] = s2[slot].astype(BF16)

    def compute(c, n_rows):
        rows = slice(0, n_rows)
        col = pl.multiple_of(c * (2 * fc), 2 * fc)
        hcat = _dot(xb_sc[rows, :], w1b[c]) + b1_ref[0, :, pl.ds(col, 2 * fc)]
        acts = []
        for g in range(2 * fc // PAIR_W):
            glu = jnp.minimum(hcat[:, g * PAIR_W:g * PAIR_W + LANES], SWIGLU_LIMIT)
            lin = jnp.clip(hcat[:, g * PAIR_W + LANES:(g + 1) * PAIR_W], -SWIGLU_LIMIT, SWIGLU_LIMIT)
            acts.append(glu * jax.nn.sigmoid(SWIGLU_ALPHA * glu) * (lin + 1.0))
        acc_sc[rows, :] += _dot(jnp.concatenate(acts, axis=1).astype(BF16), w2b[c])

    def chunk_loop(first, n_rows):
        def body(c, carry):
            if first:
                slot = c % 2
                weights_wait(e, c, slot)
                convert(c, slot)

                @pl.when(c + 2 < n_chunks)
                def _():
                    weights_start(e, c + 2, slot)

                @pl.when((c + 2 >= n_chunks) & (nxok_ref[i] > 0))
                def _():
                    weights_start(nxe_ref[i], c + 2 - n_chunks, slot)
            compute(c, n_rows)
            return carry
        lax.fori_loop(0, n_chunks, body, 0)

    @pl.when(valid)
    def _():
        _gather_wait(xbuf, 0, gsem)
        for r0 in range(0, bm, half):
            xb_sc[r0:r0 + half, :] = _tiles_to_rows(xbuf, 0, r0, half, n_sub).astype(BF16)
        nxt = jnp.minimum(i + 1, nb - 1)

        @pl.when((i + 1 < nb) & (bv_ref[nxt] > 0))
        def _():
            _gather_rows(x_hbm, tok_ref, (i + 1) * bm, xbuf, 0, gsem, bm, n_sub)

        acc_sc[...] = jnp.broadcast_to(b2_ref[0], acc_sc.shape)
        is_first = bf_ref[i] > 0
        n_valid = br_ref[i]
        sizes = (bm, half, half // 2)
        for first in (True, False):
            for k, n_rows in enumerate(sizes):
                cond_first = is_first if first else jnp.logical_not(is_first)
                cond_rows = n_valid <= n_rows
                if k + 1 < len(sizes):
                    cond_rows = cond_rows & (n_valid > sizes[k + 1])
                if k == 0:
                    cond_rows = n_valid > sizes[1]

                @pl.when(cond_first & cond_rows)
                def _():
                    chunk_loop(first, n_rows)

        for r0 in range(0, bm, half):
            y = acc_sc[r0:r0 + half, :] * g_ref[r0:r0 + half, :]
            for s in range(n_sub):
                o_ref[pl.ds(r0 * n_sub + s, half, stride=n_sub), :] = y[:, LANES * s:LANES * (s + 1)]

    @pl.when(jnp.logical_not(valid))
    def _():
        o_ref[...] = jnp.zeros(o_ref.shape, F32)


def _moe(route, x_tiles, w1, b1g, w2, b2, bm, fc):
    E, D, F2 = w1.shape
    n_chunks = F2 // (2 * fc)
    assert n_chunks % 2 == 0 and (2 * fc) % PAIR_W == 0 and D % CONV_ROWS == 0
    n_sub = x_tiles.shape[1]
    n_blocks = route['block_expert'].shape[0]
    col = jnp.arange(PAIR_W)
    src = jnp.where(col < LANES, 2 * col, 2 * (col - LANES) + 1)
    perm = (jnp.arange(PAIR_W)[:, None] == src[None, :]).astype(BF16)
    by_expert = lambda i, be, *_: (be[i], 0, 0)
    return pl.pallas_call(
        functools.partial(_moe_kernel, bm=bm, fc=fc, n_chunks=n_chunks, n_sub=n_sub),
        grid_spec=pltpu.PrefetchScalarGridSpec(
            num_scalar_prefetch=7,
            grid=(n_blocks,),
            in_specs=[
                pl.BlockSpec(memory_space=pl.ANY),
                pl.BlockSpec(memory_space=pl.ANY),
                pl.BlockSpec(memory_space=pl.ANY),
                pl.BlockSpec((PAIR_W, PAIR_W), lambda i, *_: (0, 0)),
                pl.BlockSpec((1, 1, F2), by_expert),
                pl.BlockSpec((1, 1, D), by_expert),
                pl.BlockSpec((bm, 1), lambda i, *_: (i, 0)),
            ],
            out_specs=pl.BlockSpec((bm * n_sub, LANES), lambda i, *_: (i, 0)),
            scratch_shapes=[
                pltpu.VMEM((1, bm * n_sub, LANES), F32), pltpu.VMEM((bm, D), BF16), pltpu.VMEM((bm, D), F32),
                pltpu.VMEM((n_chunks, D, 2 * fc), BF16), pltpu.VMEM((n_chunks, fc, D), BF16),
                pltpu.VMEM((2, D, 2 * fc), F32), pltpu.VMEM((2, fc, D), F32),
                pltpu.SemaphoreType.DMA((1,)), pltpu.SemaphoreType.DMA((2, 2)),
            ],
        ),
        out_shape=jax.ShapeDtypeStruct((n_blocks * bm * n_sub, LANES), F32),
        compiler_params=pltpu.CompilerParams(dimension_semantics=("arbitrary",), vmem_limit_bytes=MOE_VMEM_LIMIT),
        name="moe_experts",
    )(route['block_expert'], route['block_valid'], route['block_first'], route['block_rows'],
      route['next_expert'], route['next_ok'], route['row_tok'], x_tiles, w1, w2, perm, b1g, b2,
      route['row_gate']).reshape(n_blocks * bm, n_sub, LANES)


def _combine_kernel(pos_ref, y_hbm, h_ref, o_ref, buf, sem, *, tm, top_k, n_sub):
    i = pl.program_id(0)
    slot = i % 2
    n = top_k * tm

    @pl.when(i == 0)
    def _():
        _gather_rows(y_hbm, pos_ref, 0, buf, 0, sem, n, n_sub)

    @pl.when(i + 1 < pl.num_programs(0))
    def _():
        _gather_rows(y_hbm, pos_ref, (i + 1) * n, buf, 1 - slot, sem, n, n_sub)

    _gather_wait(buf, slot, sem)
    acc = h_ref[...]
    for k in range(top_k):
        acc = acc + _tiles_to_rows(buf, slot, k * tm, tm, n_sub)
    o_ref[...] = acc


def _combine(pos, y_tiles, h, tm):
    T, D = h.shape
    n_sub = y_tiles.shape[1]
    return pl.pallas_call(
        functools.partial(_combine_kernel, tm=tm, top_k=TOP_K, n_sub=n_sub),
        grid_spec=pltpu.PrefetchScalarGridSpec(
            num_scalar_prefetch=1,
            grid=(T // tm,),
            in_specs=[
                pl.BlockSpec(memory_space=pl.ANY),
                pl.BlockSpec((tm, D), lambda i, *_: (i, 0)),
            ],
            out_specs=pl.BlockSpec((tm, D), lambda i, *_: (i, 0)),
            scratch_shapes=[pltpu.VMEM((2, TOP_K * tm * n_sub, LANES), F32), pltpu.SemaphoreType.DMA((2,))],
        ),
        out_shape=jax.ShapeDtypeStruct((T, D), F32),
        compiler_params=_cparams(("arbitrary",)),
        name="moe_combine",
    )(pos, y_tiles, h)


def _routing(top_idx, top_gate, counts, bm, tc):
    T = top_idx.shape[1]
    N = TOP_K * T
    n_experts = counts.shape[0]
    i32 = jnp.int32
    iota = jnp.arange(N, dtype=i32)
    experts = jnp.arange(n_experts, dtype=i32)
    e_sorted, order, gate_sorted = lax.sort((top_idx.reshape(-1), iota, top_gate.reshape(-1)), num_keys=1)
    starts = jnp.cumsum(counts) - counts
    padded = ((counts + bm - 1) // bm) * bm
    pad_ends = jnp.cumsum(padded)
    pad_starts = pad_ends - padded
    dest_sorted = (pad_starts - starts)[e_sorted] + iota
    _, dest = lax.sort_key_val(order, dest_sorted)
    pos = dest.reshape(TOP_K, T // tc, tc).transpose(1, 0, 2).reshape(-1)

    n_blocks = -(-N // bm) + n_experts
    blk_start = jnp.arange(n_blocks, dtype=i32) * bm
    blk_e = jnp.minimum(jnp.sum((pad_ends[None, :] <= blk_start[:, None]).astype(i32), axis=1), n_experts - 1)
    block_valid = (blk_start < pad_ends[-1]).astype(i32)
    blk_off = blk_start - pad_starts[blk_e]
    block_rows = block_valid * jnp.clip(counts[blk_e] - blk_off, 0, bm)
    prev_e = jnp.concatenate([jnp.full((1,), -1, i32), blk_e[:-1]])
    block_first = block_valid * (blk_e != prev_e).astype(i32)
    last_e = jnp.max(jnp.where(counts > 0, experts, 0))
    block_expert = jnp.where(block_valid > 0, blk_e, last_e)
    later = lax.cummin(jnp.where(counts > 0, experts, n_experts)[::-1])[::-1]
    nxt = jnp.concatenate([later[1:], jnp.full((1,), n_experts, i32)])[block_expert]
    next_ok = (nxt < n_experts).astype(i32)
    next_expert = jnp.minimum(nxt, n_experts - 1)

    in_blk = jnp.arange(bm, dtype=i32)[None, :]
    row_ok = (in_blk < block_rows[:, None]).reshape(-1)
    src = jnp.clip((starts[blk_e] + blk_off)[:, None] + in_blk, 0, N - 1).reshape(-1)
    row_tok = jnp.where(row_ok, (order % T)[src], 0)
    row_gate = jnp.where(row_ok, gate_sorted[src], 0.0)
    return dict(row_tok=row_tok, row_gate=row_gate.reshape(-1, 1), pos=pos, block_expert=block_expert,
                block_valid=block_valid, block_first=block_first, block_rows=block_rows,
                next_expert=next_expert, next_ok=next_ok)


def _tile(n, pref):
    for t in pref:
        if n % t == 0:
            return t
    raise ValueError(f"no tile for {n}")


def _layer(layer, xp, xs, cache_k, cache_v, state_conv, page_table, p):
    n_b, seq, D = xp.shape
    dec_b, dec_seq, _ = xs.shape
    assert n_b == 1
    _, n_phys, page, n_heads, _, head_dim = cache_k.shape
    attn_w = n_heads * 2 * head_dim
    conv_w = p['w_conv'].shape[1]
    assert conv_w == attn_w
    n_exp = p['w_router'].shape[1]
    d_ff = p['w_mlp2'].shape[1]
    Tp, Ts = n_b * seq, dec_b * dec_seq
    T = Tp + Ts
    lam0 = 0.8 - 0.6 * math.exp(-0.3 * layer)

    x = jnp.concatenate([xp.reshape(Tp, D), xs.reshape(Ts, D)], axis=0)

    tm = _tile(math.gcd(Tp, Ts), (512, 256, 128))
    proj = _in_proj(x, p['norm1_g'], p['w_in'].astype(BF16), tm, 1024)

    tm2 = _tile(math.gcd(Tp, Ts), (256, 128))
    t_idx = jnp.arange(T, dtype=jnp.int32)
    pos_in_seq = jnp.where(t_idx < Tp, t_idx, (t_idx - Tp) % dec_seq)
    m1 = (pos_in_seq >= 1).astype(F32).reshape(T, 1)
    m2 = (pos_in_seq >= 2).astype(F32).reshape(T, 1)
    st = state_conv.astype(F32)
    zero = jnp.zeros((dec_b, dec_seq - 1, conv_w), F32)
    s1 = jnp.concatenate([st[:, 1:2], zero], axis=1).reshape(Ts, conv_w)
    s2 = jnp.concatenate([st[:, 0:2], zero[:, 1:]], axis=1).reshape(Ts, conv_w)
    reps = attn_w // head_dim
    gq = jnp.tile(p['q_norm_g'].astype(F32), reps).reshape(1, attn_w)
    gk = jnp.tile(p['k_norm_g'].astype(F32), reps).reshape(1, attn_w)
    lane = jnp.arange(LANES)
    gmat = ((lane[:, None] // head_dim == lane[None, :] // head_dim).astype(F32) / head_dim).astype(BF16)
    f32 = F32
    slopes = 2.0 ** (-8.0 * jnp.arange(1, n_heads + 1, dtype=f32) / n_heads)
    a_log2 = slopes * LOG2E
    a_hi, a_lo = _split_bf16(a_log2)
    off = jnp.arange(attn_w) % (2 * head_dim)
    a_hi_l, a_lo_l = jnp.repeat(a_hi.astype(f32), 2 * head_dim), jnp.repeat(a_lo.astype(f32), 2 * head_dim)
    aug_row = lambda o: jnp.where(o < 0, 0.0, jnp.where(o < 2, a_hi_l, jnp.where(o < 4, a_lo_l, 0.0)))
    qaug = jnp.stack([aug_row(off - head_dim), aug_row(off)]).reshape(2, 1, attn_w)
    blk = _tile(Tp, (1024, 512, 256, 128))
    ya, u, qa, ka, qs, kf, vb = _mix_prep(proj, m1, m2, s1, s2, p['w_conv'].astype(F32), gq, gk, gmat, qaug,
                                         tm2, Tp // tm2, head_dim, blk)
    vf = proj[:, 3 * conv_w + 2 * attn_w:3 * conv_w + 3 * attn_w]

    lam = (jnp.exp(jnp.sum(p['lambda_q1'].astype(f32) * p['lambda_k1'].astype(f32)))
           - jnp.exp(jnp.sum(p['lambda_q2'].astype(f32) * p['lambda_k2'].astype(f32))) + lam0).reshape(1)
    g_sub = p['subln_g'].astype(F32).reshape(1, 2 * head_dim)
    yb_p = _prompt_attn(a_hi.astype(f32) + a_lo.astype(f32), lam, qa, ka, vb, g_sub, Tp, n_heads, blk, lam0)

    pad_rows = lambda a: jnp.pad(a.reshape(dec_b, dec_seq, attn_w), ((0, 0), (0, QPAD - dec_seq), (0, 0)))
    q_s = pad_rows(qs[Tp:])
    yb_s = _paged_attn(page_table, lam, q_s,
                       jnp.transpose(cache_k, (0, 1, 3, 4, 5, 2)), cache_v, layer,
                       pad_rows(kf[Tp:]), pad_rows(vf[Tp:]), g_sub, dec_seq, n_heads, head_dim, lam0)
    yb = jnp.concatenate([yb_p, yb_s[:, :dec_seq].reshape(Ts, attn_w)], axis=0)

    wr = p['w_router'].astype(F32).T
    wrh, wrl = _split_bf16(wr)
    h, xn2, top_idx, top_gate, expert_counts = _out_proj(
        ya, yb, proj, x, p['w_proj_a'].astype(BF16), p['w_proj_b'].astype(BF16), p['w_o'].astype(BF16),
        p['norm2_g'].astype(F32).reshape(1, D), wrh, wrl, p['b_router'].astype(F32).reshape(n_exp, 1),
        tm2, (3 * conv_w + 3 * attn_w) // D)

    bm, fc, tc = MOE_BM, MOE_FC, MOE_TC
    route = _routing(top_idx, top_gate, expert_counts.reshape(-1).astype(jnp.int32), bm, tc)
    b1g = (p['b_mlp1'].astype(F32).reshape(n_exp, 2 * d_ff // PAIR_W, LANES, 2).transpose(0, 1, 3, 2)
           .reshape(n_exp, 1, 2 * d_ff))
    yr = _moe(route, xn2.reshape(T, D // LANES, LANES), p['w_mlp1'].astype(F32), b1g,
              p['w_mlp2'].astype(F32), p['b_mlp2'].astype(F32).reshape(n_exp, 1, D), bm, fc)
    y = _combine(route['pos'], yr, h, tc)

    k_p = kf[:Tp].reshape(n_b, seq, n_heads, 2, head_dim)
    v_p = vf[:Tp].reshape(n_b, seq, n_heads, 2 * head_dim)
    c_p = u[Tp - (CONV_K - 1):Tp].reshape(n_b, CONV_K - 1, conv_w)
    k_s = kf[Tp:].reshape(dec_b, dec_seq, n_heads, 2, head_dim)
    v_s = vf[Tp:].reshape(dec_b, dec_seq, n_heads, 2 * head_dim)
    c_s = u[Tp:].reshape(dec_b, dec_seq, conv_w)[:, dec_seq - (CONV_K - 1):]
    return (y[:Tp].reshape(n_b, seq, D), y[Tp:].reshape(dec_b, dec_seq, D), k_p, v_p, c_p, k_s, v_s, c_s)


def kernel(x_prompt, x_sample, cache_k, cache_v, state_conv, page_table, norm1_g, w_in, w_conv, q_norm_g, k_norm_g, lambda_q1, lambda_k1, lambda_q2, lambda_k2, subln_g, w_proj_a, w_proj_b, w_o, norm2_g, w_router, b_router, w_mlp1, b_mlp1, w_mlp2, b_mlp2):
    depth = w_in.shape[0]
    names = ('norm1_g', 'w_in', 'w_conv', 'q_norm_g', 'k_norm_g', 'lambda_q1', 'lambda_k1', 'lambda_q2',
             'lambda_k2', 'subln_g', 'w_proj_a', 'w_proj_b', 'w_o', 'norm2_g', 'w_router', 'b_router',
             'w_mlp1', 'b_mlp1', 'w_mlp2', 'b_mlp2')
    stacked = (norm1_g, w_in, w_conv, q_norm_g, k_norm_g, lambda_q1, lambda_k1, lambda_q2, lambda_k2,
               subln_g, w_proj_a, w_proj_b, w_o, norm2_g, w_router, b_router, w_mlp1, b_mlp1, w_mlp2, b_mlp2)
    h_p, h_s = x_prompt, x_sample
    outs = [[] for _ in range(6)]
    for l in range(depth):
        p = {n: a[l] for n, a in zip(names, stacked)}
        res = _layer(l, h_p, h_s, cache_k, cache_v, state_conv[l], page_table, p)
        h_p, h_s = res[0], res[1]
        for acc, r in zip(outs, res[2:]):
            acc.append(r)
    return (h_p, h_s) + tuple(jnp.stack(o) for o in outs)
```

```python
import functools
import math

import jax
import jax.numpy as jnp
from jax import lax
from jax.experimental import pallas as pl
from jax.experimental.pallas import tpu as pltpu

F32 = jnp.float32
BF16 = jnp.bfloat16

TOP_K = 4
CONV_K = 3
NORM_EPS = 1e-6
NEG_INF = -1e30
SWIGLU_ALPHA = 1.702
SWIGLU_LIMIT = 7.0
LANES = 128
VMEM_LIMIT = 56 * 1024 * 1024


def _cparams(sem, flags=None):
    return pltpu.CompilerParams(dimension_semantics=sem, vmem_limit_bytes=VMEM_LIMIT, flags=flags)


def _nt_dot(a, b):
    return lax.dot_general(a, b, (((1,), (1,)), ((), ())), preferred_element_type=F32)


def _dot(a, b):
    return jnp.dot(a, b, preferred_element_type=F32)


def _split_bf16(x):
    hi = x.astype(BF16)
    lo = (x - hi.astype(F32)).astype(BF16)
    return hi, lo


def _row_tile_specs(tm, D, n_prompt_tiles):
    return [pl.BlockSpec((tm, D), lambda i, *_: (jnp.minimum(i, n_prompt_tiles - 1), 0)),
            pl.BlockSpec((tm, D), lambda i, *_: (jnp.maximum(i - n_prompt_tiles, 0), 0))]


def _in_proj_kernel(xp_ref, xs_ref, g_ref, w_ref, o_ref, xn_sc, *, n_prompt_tiles):
    @pl.when(pl.program_id(1) == 0)
    def _():
        x = jnp.where(pl.program_id(0) < n_prompt_tiles, xp_ref[...], xs_ref[...])
        ms = jnp.mean(x * x, axis=-1, keepdims=True)
        xn_sc[...] = (x * lax.rsqrt(ms + NORM_EPS) * g_ref[...]).astype(BF16)

    o_ref[...] = _dot(xn_sc[...], w_ref[...])


def _in_proj(xp, xs, g, w_bf16, tm, tn):
    (Tp, D), Ts = xp.shape, xs.shape[0]
    N = w_bf16.shape[1]
    return pl.pallas_call(
        functools.partial(_in_proj_kernel, n_prompt_tiles=Tp // tm),
        grid=((Tp + Ts) // tm, N // tn),
        in_specs=_row_tile_specs(tm, D, Tp // tm) + [
            pl.BlockSpec((1, D), lambda i, j: (0, 0)),
            pl.BlockSpec((D, tn), lambda i, j: (0, j)),
        ],
        out_specs=pl.BlockSpec((tm, tn), lambda i, j: (i, j)),
        out_shape=jax.ShapeDtypeStruct((Tp + Ts, N), F32),
        scratch_shapes=[pltpu.VMEM((tm, D), BF16)],
        compiler_params=_cparams(("parallel", "arbitrary")),
        name="in_proj",
    )(xp, xs, g.reshape(1, D), w_bf16)


def _group_mean_sq(x, gmat):
    hi, lo = _split_bf16(x * x)
    outs = []
    for j in range(x.shape[1] // LANES):
        sl = slice(LANES * j, LANES * (j + 1))
        outs.append(_dot(hi[:, sl], gmat) + _dot(lo[:, sl], gmat))
    return jnp.concatenate(outs, axis=1)


LOG2E = math.log2(math.e)
KPOS_SPLIT = 256


def _mix_prep_kernel(cb_ref, cc_ref, cx_ref, ccp_ref, cxp_ref, q_ref, k_ref, v_ref,
                     m1_ref, m2_ref, s1_ref, s2_ref, wc_ref, gq_ref, gk_ref, gmat_ref, qaug_ref,
                     ya_ref, u_ref, qa_ref, ka_ref, qs_ref, kf_ref, vf_ref, vb_ref,
                     *, n_prompt_tiles, head_dim, kv_block):
    i = pl.program_id(0)
    u = cc_ref[...] * cx_ref[...]
    u_ref[...] = u
    up = ccp_ref[...] * cxp_ref[...]
    row = lax.broadcasted_iota(jnp.int32, u.shape, 0)
    p1 = pltpu.roll(u, 1, 0)
    p1 = jnp.where(row == 0, up[7:8, :], p1)
    p2 = pltpu.roll(u, 2, 0)
    p2 = jnp.where(row == 0, up[6:7, :], jnp.where(row == 1, up[7:8, :], p2))
    is_sample = i >= n_prompt_tiles
    s1 = jnp.where(is_sample, s1_ref[...], 0.0)
    s2 = jnp.where(is_sample, s2_ref[...], 0.0)
    p1 = p1 * m1_ref[...] + s1
    p2 = p2 * m2_ref[...] + s2
    w = wc_ref[...]
    conv = p2 * w[0:1, :] + p1 * w[1:2, :] + u * w[2:3, :]
    ya_ref[...] = (cb_ref[...] * conv).astype(BF16)

    gmat = gmat_ref[...]
    q = q_ref[...]
    qn = q * lax.rsqrt(_group_mean_sq(q, gmat) + NORM_EPS) * gq_ref[...] * (head_dim ** -0.5)
    qs_ref[...] = qn
    k = k_ref[...]
    kn = k * lax.rsqrt(_group_mean_sq(k, gmat) + NORM_EPS) * gk_ref[...]
    kf_ref[...] = kn
    vf_ref[...] = v_ref[...]
    vb_ref[...] = v_ref[...].astype(BF16)

    lane = lax.broadcasted_iota(jnp.int32, q.shape, 1)
    off = lane % (2 * head_dim)
    first_map = off < head_dim
    pos = (i * q.shape[0] + lax.broadcasted_iota(jnp.int32, (q.shape[0], 1), 0)) % kv_block
    pos_lo = (pos % KPOS_SPLIT).astype(F32)
    pos_hi = (pos - pos % KPOS_SPLIT).astype(F32)
    ql = qn * LOG2E
    for c in range(2):
        data = first_map if c == 0 else jnp.logical_not(first_map)
        aug = off - (1 - c) * head_dim
        k_aug = jnp.where((aug == 0) | (aug == 2), pos_lo, jnp.where((aug == 1) | (aug == 3), pos_hi, 0.0))
        qa_ref[c] = jnp.where(data, ql, qaug_ref[c]).astype(BF16)
        ka_ref[c] = jnp.where(data, kn, k_aug).astype(BF16)


def _mix_prep(proj, m1, m2, s1, s2, w_conv, gq, gk, gmat, qaug, tm, n_prompt_tiles, head_dim, kv_block):
    T = proj.shape[0]
    W = w_conv.shape[1]
    col = lambda c: pl.BlockSpec((tm, W), lambda i, c=c: (i, c))
    prev = lambda c: pl.BlockSpec((8, W), lambda i, c=c: (jnp.maximum(i * (tm // 8) - 1, 0), c))
    samp = pl.BlockSpec((tm, W), lambda i: (jnp.maximum(i - n_prompt_tiles, 0), 0))
    rowv = pl.BlockSpec((tm, 1), lambda i: (i, 0))
    full = lambda a: pl.BlockSpec(a.shape, lambda i: (0,) * a.ndim)
    out_row = pl.BlockSpec((tm, W), lambda i: (i, 0))
    both_maps = pl.BlockSpec((2, tm, W), lambda i: (0, i, 0))
    return pl.pallas_call(
        functools.partial(_mix_prep_kernel, n_prompt_tiles=n_prompt_tiles, head_dim=head_dim, kv_block=kv_block),
        grid=(T // tm,),
        in_specs=[col(0), col(1), col(2), prev(1), prev(2), col(3), col(4), col(5),
                  rowv, rowv, samp, samp, full(w_conv), full(gq), full(gk), full(gmat), full(qaug)],
        out_specs=[out_row, out_row, both_maps, both_maps, out_row, out_row, out_row, out_row],
        out_shape=[jax.ShapeDtypeStruct((T, W), BF16), jax.ShapeDtypeStruct((T, W), F32),
                   jax.ShapeDtypeStruct((2, T, W), BF16), jax.ShapeDtypeStruct((2, T, W), BF16),
                   jax.ShapeDtypeStruct((T, W), F32), jax.ShapeDtypeStruct((T, W), F32),
                   jax.ShapeDtypeStruct((T, W), F32), jax.ShapeDtypeStruct((T, W), BF16)],
        compiler_params=_cparams(("parallel",)),
        name="mix_prep",
    )(proj, proj, proj, proj, proj, proj, proj, proj, m1, m2, s1, s2, w_conv, gq, gk, gmat, qaug)


def _sub_norm(o, g, lam0):
    ms = jnp.mean(o * o, axis=-1, keepdims=True)
    return o * lax.rsqrt(ms + NORM_EPS) * g * (1.0 - lam0)


ATTN_SUB = 128


def _prompt_attn_kernel(a_ref, lam_ref, ii_ref, jj_ref, qa_ref, ka_ref, v_ref, g_ref, o_ref,
                        m_sc, acc_sc, *, blk, sub, lam0):
    h, t = pl.program_id(0), pl.program_id(1)
    i, j = ii_ref[t], jj_ref[t]
    hd2 = v_ref.shape[1]

    @pl.when(j == 0)
    def _():
        m_sc[...] = jnp.full(m_sc.shape, NEG_INF, F32)
        acc_sc[...] = jnp.zeros(acc_sc.shape, F32)

    v_ones = jnp.concatenate([v_ref[...], jnp.ones((blk, hd2), BF16)], axis=1)

    def update(masked):
        for r0 in range(0, blk, sub):
            rows = pl.ds(r0, sub)
            row = r0 + lax.broadcasted_iota(jnp.int32, (sub, 1), 0)
            row_term = a_ref[h] * ((i - j) * blk + row).astype(F32)
            for c in range(2):
                s = _nt_dot(qa_ref[c, rows, :], ka_ref[c])
                if masked:
                    s = jnp.where(lax.broadcasted_iota(jnp.int32, s.shape, 1) <= row, s, NEG_INF)
                m_prev = m_sc[c, rows, :]
                m_new = jnp.maximum(m_prev, jnp.max(s, axis=1, keepdims=True) - row_term)
                p = jnp.exp2(s - (row_term + m_new)).astype(BF16)
                acc_sc[c, rows, :] = jnp.exp2(m_prev - m_new) * acc_sc[c, rows, :] + _dot(p, v_ones)
                m_sc[c, rows, :] = m_new

    @pl.when(j < i)
    def _():
        update(False)

    @pl.when(j == i)
    def _():
        update(True)
        a0, a1 = acc_sc[0], acc_sc[1]
        o = a0[:, :hd2] / a0[:, hd2:] - lam_ref[0] * (a1[:, :hd2] / a1[:, hd2:])
        o_ref[...] = _sub_norm(o, g_ref[...], lam0).astype(BF16)


def _prompt_attn(a_log2, lam, qa, ka, vb, g_sub, n_rows, n_heads, blk, lam0):
    hd2 = g_sub.shape[1]
    nb = n_rows // blk
    pairs = [(i, j) for i in range(nb) for j in range(i + 1)]
    ii = jnp.asarray([p[0] for p in pairs], jnp.int32)
    jj = jnp.asarray([p[1] for p in pairs], jnp.int32)
    return pl.pallas_call(
        functools.partial(_prompt_attn_kernel, blk=blk, sub=min(blk, ATTN_SUB), lam0=lam0),
        grid_spec=pltpu.PrefetchScalarGridSpec(
            num_scalar_prefetch=4,
            grid=(n_heads, len(pairs)),
            in_specs=[
                pl.BlockSpec((2, blk, hd2), lambda h, t, a, l, ii, jj: (0, ii[t], h)),
                pl.BlockSpec((2, blk, hd2), lambda h, t, a, l, ii, jj: (0, jj[t], h)),
                pl.BlockSpec((blk, hd2), lambda h, t, a, l, ii, jj: (jj[t], h)),
                pl.BlockSpec((1, hd2), lambda h, t, *_: (0, 0)),
            ],
            out_specs=pl.BlockSpec((blk, hd2), lambda h, t, a, l, ii, jj: (ii[t], h)),
            scratch_shapes=[pltpu.VMEM((2, blk, 1), F32), pltpu.VMEM((2, blk, 2 * hd2), F32)],
        ),
        out_shape=jax.ShapeDtypeStruct((n_rows, n_heads * hd2), BF16),
        compiler_params=_cparams(("parallel", "arbitrary")),
        name="prompt_attn",
    )(a_log2, lam, ii, jj, qa, ka, vb, g_sub)


QPAD = 8


def _paged_attn_kernel(pt_ref, lam_ref, q_ref, *rest, n_pages, pps, page, n_new, n_heads, head_dim, lam0):
    kc_refs, vc_refs = rest[:pps], rest[pps:2 * pps]
    kn_ref, vn_ref, g_ref, o_ref, qbd_sc, m_sc, l_sc, acc_sc = rest[2 * pps:]
    p = pl.program_id(1)
    n_steps = n_pages // pps
    n_maps = 2 * n_heads
    rows = n_maps * QPAD
    width = n_maps * head_dim
    r = lax.broadcasted_iota(jnp.int32, (rows, 1), 0)
    qi = r % QPAD
    slope = jnp.exp2(-(r // (2 * QPAD) + 1).astype(F32))

    @pl.when(p == 0)
    def _():
        q = jnp.tile(q_ref[0], (n_maps, 1))
        rr = lax.broadcasted_iota(jnp.int32, (rows, width), 0)
        ll = lax.broadcasted_iota(jnp.int32, (rows, width), 1)
        qbd_sc[...] = jnp.where(ll // head_dim == rr // QPAD, q, 0.0).astype(BF16)
        m_sc[...] = jnp.full(m_sc.shape, NEG_INF, F32)
        l_sc[...] = jnp.zeros(l_sc.shape, F32)
        acc_sc[...] = jnp.zeros(acc_sc.shape, F32)

    def step(s, vblk):
        m_prev = m_sc[...]
        m_new = jnp.maximum(m_prev, jnp.max(s, axis=1, keepdims=True))
        pr = jnp.exp(s - m_new)
        alpha = jnp.exp(m_prev - m_new)
        l_sc[...] = alpha * l_sc[...] + jnp.sum(pr, axis=1, keepdims=True)
        acc_sc[...] = alpha * acc_sc[...] + _dot(pr.astype(BF16), vblk)
        m_sc[...] = m_new

    past = n_pages * page

    @pl.when(p < n_steps)
    def _():
        key = lax.broadcasted_iota(jnp.int32, (rows, pps * page), 1)
        dist = (past + qi - (p * (pps * page) + key)).astype(F32)
        kt = jnp.concatenate([kc[0, 0].reshape(width, page).astype(BF16) for kc in kc_refs], axis=1)
        vblk = jnp.concatenate(
            [jnp.concatenate([vc[0, 0, pl.ds(hh, page, stride=n_heads), :] for hh in range(n_heads)],
                             axis=1).astype(BF16) for vc in vc_refs], axis=0)
        step(_dot(qbd_sc[...], kt) - slope * dist, vblk)

    @pl.when(p == n_steps)
    def _():
        key = lax.broadcasted_iota(jnp.int32, (rows, page), 1)
        zeros = jnp.zeros((page - QPAD, width), F32)
        kblk = jnp.concatenate([kn_ref[0], zeros], axis=0).astype(BF16)
        vblk = jnp.concatenate([vn_ref[0], zeros], axis=0).astype(BF16)
        dist = (qi - key).astype(F32)
        s = _nt_dot(qbd_sc[...], kblk) - slope * dist
        step(jnp.where((key <= qi) & (key < n_new), s, NEG_INF), vblk)

        c = r // QPAD
        wgt = jnp.where(c % 2 == 0, 1.0, -lam_ref[0]) / l_sc[...]
        rr = lax.broadcasted_iota(jnp.int32, (rows, width), 0)
        ll = lax.broadcasted_iota(jnp.int32, (rows, width), 1)
        own = ll // (2 * head_dim) == rr // (2 * QPAD)
        accw = jnp.where(own, acc_sc[...] * wgt, 0.0)
        o = jnp.sum(accw.reshape(n_maps, QPAD, width), axis=0)
        g = g_ref[...]
        hd2 = 2 * head_dim
        o_ref[0] = jnp.concatenate(
            [_sub_norm(o[:, hd2 * hh:hd2 * (hh + 1)], g, lam0) for hh in range(n_heads)],
            axis=1).astype(BF16)


def _paged_attn(page_table, lam, q_pad, cache_kt, cache_v, layer, k_new, v_new, g_sub,
                n_new, n_heads, head_dim, lam0):
    B, n_pages = page_table.shape
    page = cache_v.shape[2]
    width = 2 * n_heads * head_dim
    rows = 2 * n_heads * QPAD
    pps = next(c for c in (8, 4, 2, 1) if n_pages % c == 0)
    n_steps = n_pages // pps

    def page_of(b, p, pt, s):
        return pt[b, jnp.minimum(p, n_steps - 1) * pps + s]

    k_specs = [pl.BlockSpec((1, 1, n_heads, 2, head_dim, page),
                            lambda b, p, pt, *_, s=s: (layer, page_of(b, p, pt, s), 0, 0, 0, 0)) for s in range(pps)]
    v_rows = cache_v.reshape(cache_v.shape[0], cache_v.shape[1], page * n_heads, 2 * head_dim)
    v_specs = [pl.BlockSpec((1, 1, page * n_heads, 2 * head_dim),
                            lambda b, p, pt, *_, s=s: (layer, page_of(b, p, pt, s), 0, 0)) for s in range(pps)]
    seq_blk = pl.BlockSpec((1, QPAD, width), lambda b, p, *_: (b, 0, 0))
    return pl.pallas_call(
        functools.partial(_paged_attn_kernel, n_pages=n_pages, pps=pps, page=page, n_new=n_new,
                          n_heads=n_heads, head_dim=head_dim, lam0=lam0),
        grid_spec=pltpu.PrefetchScalarGridSpec(
            num_scalar_prefetch=2,
            grid=(B, n_steps + 1),
            in_specs=[seq_blk] + k_specs + v_specs + [
                seq_blk, seq_blk,
                pl.BlockSpec((1, 2 * head_dim), lambda b, p, *_: (0, 0)),
            ],
            out_specs=seq_blk,
            scratch_shapes=[pltpu.VMEM((rows, width), BF16), pltpu.VMEM((rows, 1), F32),
                            pltpu.VMEM((rows, 1), F32), pltpu.VMEM((rows, width), F32)],
        ),
        out_shape=jax.ShapeDtypeStruct((B, QPAD, width), BF16),
        compiler_params=_cparams(("parallel", "arbitrary")),
        name="paged_attn",
    )(page_table, lam, q_pad, *([cache_kt] * pps), *([v_rows] * pps), k_new, v_new, g_sub)


def _out_proj_kernel(ya_ref, yb_ref, ga_ref, gb_ref, xp_ref, xs_ref, wa_ref, wb_ref, wo_ref, g2_ref,
                     wrh_ref, wrl_ref, br_ref, h_ref, xn_ref, ti_ref, tg_ref, cnt_ref,
                     *, top_k, n_prompt_tiles):
    pa = _dot(ya_ref[...], wa_ref[...])
    pb = _dot(yb_ref[...], wb_ref[...])
    merged = jax.nn.sigmoid(ga_ref[...]) * pa + jax.nn.sigmoid(gb_ref[...]) * pb
    x = jnp.where(pl.program_id(0) < n_prompt_tiles, xp_ref[...], xs_ref[...])
    h = x + _dot(merged.astype(BF16), wo_ref[...])
    h_ref[...] = h
    ms = jnp.mean(h * h, axis=-1, keepdims=True)
    xn = h * lax.rsqrt(ms + NORM_EPS) * g2_ref[...]
    xn_ref[...] = xn

    xh, xl = _split_bf16(xn)
    wrh = wrh_ref[...]
    lg = _nt_dot(wrh, xh) + _nt_dot(wrh, xl) + _nt_dot(wrl_ref[...], xh) + br_ref[...]
    n_exp = lg.shape[0]
    row = lax.broadcasted_iota(jnp.int32, lg.shape, 0)
    vals, idxs = [], []
    chosen = jnp.zeros(lg.shape, F32)
    for _ in range(top_k):
        mx = jnp.max(lg, axis=0, keepdims=True)
        idx = jnp.min(jnp.where(lg == mx, row, n_exp), axis=0, keepdims=True)
        vals.append(mx)
        idxs.append(idx)
        chosen = chosen + (row == idx).astype(F32)
        lg = jnp.where(row == idx, -jnp.inf, lg)
    es = [jnp.exp(v - vals[0]) for v in vals]
    den = es[0]
    for e in es[1:]:
        den = den + e
    ti_ref[...] = jnp.concatenate(idxs, axis=0)
    tg_ref[...] = jnp.concatenate([e / den for e in es], axis=0)

    @pl.when(pl.program_id(0) == 0)
    def _():
        cnt_ref[...] = jnp.zeros(cnt_ref.shape, F32)

    cnt_ref[...] += jnp.sum(chosen, axis=1, keepdims=True)


def _out_proj(ya, yb, proj, xp, xs, wa, wb, wo, g2, wrh, wrl, br, tm, gate_col):
    (Tp, D), T = xp.shape, ya.shape[0]
    W = ya.shape[1]
    E = wrh.shape[0]
    const = lambda a: pl.BlockSpec(a.shape, lambda i: (0,) * a.ndim, pipeline_mode=pl.Buffered(1))
    row = lambda w: pl.BlockSpec((tm, w), lambda i: (i, 0))
    return pl.pallas_call(
        functools.partial(_out_proj_kernel, top_k=TOP_K, n_prompt_tiles=Tp // tm),
        grid=(T // tm,),
        in_specs=[row(W), row(W),
                  pl.BlockSpec((tm, D), lambda i: (i, gate_col)),
                  pl.BlockSpec((tm, D), lambda i: (i, gate_col + 1))]
                 + _row_tile_specs(tm, D, Tp // tm)
                 + [const(wa), const(wb), const(wo), const(g2), const(wrh), const(wrl), const(br)],
        out_specs=[row(D), row(D),
                   pl.BlockSpec((TOP_K, tm), lambda i: (0, i)),
                   pl.BlockSpec((TOP_K, tm), lambda i: (0, i)),
                   pl.BlockSpec((E, 1), lambda i: (0, 0))],
        out_shape=[jax.ShapeDtypeStruct((T, D), F32), jax.ShapeDtypeStruct((T, D), F32),
                   jax.ShapeDtypeStruct((TOP_K, T), jnp.int32),
                   jax.ShapeDtypeStruct((TOP_K, T), F32),
                   jax.ShapeDtypeStruct((E, 1), F32)],
        compiler_params=_cparams(("arbitrary",)),
        name="out_proj_router",
    )(ya, yb, proj, proj, xp, xs, wa, wb, wo, g2, wrh, wrl, br)


DMA_UNROLL = 8


def _tiles_to_rows(ref, slot, first, n, n_sub):
    return jnp.concatenate([ref[slot, pl.ds(first * n_sub + s, n, stride=n_sub), :] for s in range(n_sub)], axis=1)


def _gather_rows(src_hbm, idx_ref, idx0, dst, slot, sem, n, n_sub):
    def body(r, c):
        pltpu.make_async_copy(src_hbm.at[idx_ref[idx0 + r]],
                              dst.at[slot, pl.ds(pl.multiple_of(r * n_sub, n_sub), n_sub)], sem.at[slot]).start()
        return c
    lax.fori_loop(0, n, body, 0, unroll=DMA_UNROLL)


def _gather_wait(dst, slot, sem):
    pltpu.make_async_copy(dst.at[slot], dst.at[slot], sem.at[slot]).wait()


MOE_BM = 576
MOE_FC = 256
MOE_TC = 128
MOE_VMEM_LIMIT = 59 * 1024 * 1024
PAIR_W = 2 * LANES
CONV_ROWS = 256


def _moe_kernel(be_ref, bv_ref, bf_ref, br_ref, nxe_ref, nxok_ref, tok_ref,
                x_hbm, w1_hbm, w2_hbm, perm_ref, b1_ref, b2_ref, g_ref, o_ref,
                xbuf, xb_sc, acc_sc, w1b, w2b, s1, s2, gsem, wsem, *, bm, fc, n_chunks, n_sub):
    i = pl.program_id(0)
    nb = pl.num_programs(0)
    valid = bv_ref[i] > 0
    e = be_ref[i]
    d_model = xb_sc.shape[1]
    half = bm // 2

    def weight_copies(ex, c, slot):
        col = pl.multiple_of(c * (2 * fc), 2 * fc)
        row = pl.multiple_of(c * fc, fc)
        return (pltpu.make_async_copy(w1_hbm.at[ex, :, pl.ds(col, 2 * fc)], s1.at[slot], wsem.at[0, slot]),
                pltpu.make_async_copy(w2_hbm.at[ex, pl.ds(row, fc), :], s2.at[slot], wsem.at[1, slot]))

    def weights_start(ex, c, slot):
        for cp in weight_copies(ex, c, slot):
            cp.start()

    def weights_wait(ex, c, slot):
        for cp in weight_copies(ex, c, slot):
            cp.wait()

    @pl.when(i == 0)
    def _():
        _gather_rows(x_hbm, tok_ref, 0, xbuf, 0, gsem, bm, n_sub)
        weights_start(e, 0, 0)
        weights_start(e, 1, 1)

    def convert(c, slot):
        perm = perm_ref[...]
        for r in range(d_model // CONV_ROWS):
            rows = slice(r * CONV_ROWS, (r + 1) * CONV_ROWS)
            wt = s1[slot, rows, :].astype(BF16)
            for g in range(2 * fc // PAIR_W):
                cols = slice(g * PAIR_W, (g + 1) * PAIR_W)
                w1b[c, rows, cols] = _dot(wt[:, cols], perm).astype(BF16)
        w2b[c] = s2[slot].astype(BF16)

    def compute(c, n_rows, init):
        rows = slice(0, n_rows)
        col = pl.multiple_of(c * (2 * fc), 2 * fc)
        hcat = _dot(xb_sc[rows, :], w1b[c]) + b1_ref[0, :, pl.ds(col, 2 * fc)]
        acts = []
        for g in range(2 * fc // PAIR_W):
            glu = jnp.minimum(hcat[:, g * PAIR_W:g * PAIR_W + LANES], SWIGLU_LIMIT)
            lin = jnp.clip(hcat[:, g * PAIR_W + LANES:(g + 1) * PAIR_W], -SWIGLU_LIMIT, SWIGLU_LIMIT)
            acts.append(glu * jax.nn.sigmoid(SWIGLU_ALPHA * glu) * (lin + 1.0))
        part = _dot(jnp.concatenate(acts, axis=1).astype(BF16), w2b[c])
        acc_sc[rows, :] = (b2_ref[0] if init else acc_sc[rows, :]) + part

    def chunk_loop(first, n_rows):
        def body(c, init):
            if first:
                slot = c % 2
                weights_wait(e, c, slot)
                convert(c, slot)

                @pl.when(c + 2 < n_chunks)
                def _():
                    weights_start(e, c + 2, slot)

                @pl.when((c + 2 >= n_chunks) & (nxok_ref[i] > 0))
                def _():
                    weights_start(nxe_ref[i], c + 2 - n_chunks, slot)
            compute(c, n_rows, init)

        body(0, True)

        def rest(c, carry):
            body(c, False)
            return carry
        lax.fori_loop(1, n_chunks, rest, 0)

    @pl.when(valid)
    def _():
        n_valid = br_ref[i]
        quarter = half // 2

        def computed(r0):
            return n_valid > (0 if r0 == 0 else quarter if r0 < half else half)

        _gather_wait(xbuf, 0, gsem)
        for r0 in range(0, bm, quarter):
            @pl.when(computed(r0))
            def _():
                xb_sc[r0:r0 + quarter, :] = _tiles_to_rows(xbuf, 0, r0, quarter, n_sub).astype(BF16)
        nxt = jnp.minimum(i + 1, nb - 1)

        @pl.when((i + 1 < nb) & (bv_ref[nxt] > 0))
        def _():
            _gather_rows(x_hbm, tok_ref, (i + 1) * bm, xbuf, 0, gsem, bm, n_sub)

        is_first = bf_ref[i] > 0
        sizes = (bm, half, quarter)
        for first in (True, False):
            for k, n_rows in enumerate(sizes):
                cond_first = is_first if first else jnp.logical_not(is_first)
                cond_rows = n_valid <= n_rows
                if k + 1 < len(sizes):
                    cond_rows = cond_rows & (n_valid > sizes[k + 1])
                if k == 0:
                    cond_rows = n_valid > sizes[1]

                @pl.when(cond_first & cond_rows)
                def _():
                    chunk_loop(first, n_rows)

        for r0 in range(0, bm, quarter):
            @pl.when(computed(r0))
            def _():
                y = acc_sc[r0:r0 + quarter, :] * g_ref[r0:r0 + quarter, :]
                for s in range(n_sub):
                    o_ref[pl.ds(r0 * n_sub + s, quarter, stride=n_sub), :] = y[:, LANES * s:LANES * (s + 1)]

            @pl.when(jnp.logical_not(computed(r0)))
            def _():
                o_ref[r0 * n_sub:(r0 + quarter) * n_sub, :] = jnp.zeros((quarter * n_sub, LANES), F32)

    @pl.when(jnp.logical_not(valid))
    def _():
        o_ref[...] = jnp.zeros(o_ref.shape, F32)


def _moe(route, x_tiles, w1, b1g, w2, b2, bm, fc):
    E, D, F2 = w1.shape
    n_chunks = F2 // (2 * fc)
    assert n_chunks % 2 == 0 and (2 * fc) % PAIR_W == 0 and D % CONV_ROWS == 0
    n_sub = x_tiles.shape[1]
    n_blocks = route['block_expert'].shape[0]
    col = jnp.arange(PAIR_W)
    src = jnp.where(col < LANES, 2 * col, 2 * (col - LANES) + 1)
    perm = (jnp.arange(PAIR_W)[:, None] == src[None, :]).astype(BF16)
    by_expert = lambda i, be, *_: (be[i], 0, 0)
    return pl.pallas_call(
        functools.partial(_moe_kernel, bm=bm, fc=fc, n_chunks=n_chunks, n_sub=n_sub),
        grid_spec=pltpu.PrefetchScalarGridSpec(
            num_scalar_prefetch=7,
            grid=(n_blocks,),
            in_specs=[
                pl.BlockSpec(memory_space=pl.ANY),
                pl.BlockSpec(memory_space=pl.ANY),
                pl.BlockSpec(memory_space=pl.ANY),
                pl.BlockSpec((PAIR_W, PAIR_W), lambda i, *_: (0, 0)),
                pl.BlockSpec((1, 1, F2), by_expert),
                pl.BlockSpec((1, 1, D), by_expert),
                pl.BlockSpec((bm, 1), lambda i, *_: (i, 0)),
            ],
            out_specs=pl.BlockSpec((bm * n_sub, LANES), lambda i, *_: (i, 0)),
            scratch_shapes=[
                pltpu.VMEM((1, bm * n_sub, LANES), F32), pltpu.VMEM((bm, D), BF16), pltpu.VMEM((bm, D), F32),
                pltpu.VMEM((n_chunks, D, 2 * fc), BF16), pltpu.VMEM((n_chunks, fc, D), BF16),
                pltpu.VMEM((2, D, 2 * fc), F32), pltpu.VMEM((2, fc, D), F32),
                pltpu.SemaphoreType.DMA((1,)), pltpu.SemaphoreType.DMA((2, 2)),
            ],
        ),
        out_shape=jax.ShapeDtypeStruct((n_blocks * bm * n_sub, LANES), F32),
        compiler_params=pltpu.CompilerParams(dimension_semantics=("arbitrary",), vmem_limit_bytes=MOE_VMEM_LIMIT),
        name="moe_experts",
    )(route['block_expert'], route['block_valid'], route['block_first'], route['block_rows'],
      route['next_expert'], route['next_ok'], route['row_tok'], x_tiles, w1, w2, perm, b1g, b2,
      route['row_gate']).reshape(n_blocks * bm, n_sub, LANES)


def _combine_kernel(pos_ref, y_hbm, h_ref, op_ref, os_ref, buf, sem, *, tm, top_k, n_sub, n_prompt_tiles):
    i = pl.program_id(0)
    slot = i % 2
    n = top_k * tm

    @pl.when(i == 0)
    def _():
        _gather_rows(y_hbm, pos_ref, 0, buf, 0, sem, n, n_sub)

    @pl.when(i + 1 < pl.num_programs(0))
    def _():
        _gather_rows(y_hbm, pos_ref, (i + 1) * n, buf, 1 - slot, sem, n, n_sub)

    _gather_wait(buf, slot, sem)
    acc = h_ref[...]
    for k in range(top_k):
        acc = acc + _tiles_to_rows(buf, slot, k * tm, tm, n_sub)

    @pl.when(i < n_prompt_tiles)
    def _():
        op_ref[...] = acc

    @pl.when(i >= n_prompt_tiles)
    def _():
        os_ref[...] = acc


def _combine(pos, y_tiles, h, tm, n_prompt):
    T, D = h.shape
    n_sub = y_tiles.shape[1]
    return pl.pallas_call(
        functools.partial(_combine_kernel, tm=tm, top_k=TOP_K, n_sub=n_sub, n_prompt_tiles=n_prompt // tm),
        grid_spec=pltpu.PrefetchScalarGridSpec(
            num_scalar_prefetch=1,
            grid=(T // tm,),
            in_specs=[
                pl.BlockSpec(memory_space=pl.ANY),
                pl.BlockSpec((tm, D), lambda i, *_: (i, 0)),
            ],
            out_specs=_row_tile_specs(tm, D, n_prompt // tm),
            scratch_shapes=[pltpu.VMEM((2, TOP_K * tm * n_sub, LANES), F32), pltpu.SemaphoreType.DMA((2,))],
        ),
        out_shape=[jax.ShapeDtypeStruct((n_prompt, D), F32), jax.ShapeDtypeStruct((T - n_prompt, D), F32)],
        compiler_params=_cparams(("arbitrary",)),
        name="moe_combine",
    )(pos, y_tiles, h)


def _routing(top_idx, top_gate, counts, bm, tc):
    T = top_idx.shape[1]
    N = TOP_K * T
    n_experts = counts.shape[0]
    i32 = jnp.int32
    iota = jnp.arange(N, dtype=i32)
    experts = jnp.arange(n_experts, dtype=i32)
    e_sorted, order, gate_sorted = lax.sort((top_idx.reshape(-1), iota, top_gate.reshape(-1)), num_keys=1)
    starts = jnp.cumsum(counts) - counts
    padded = ((counts + bm - 1) // bm) * bm
    pad_ends = jnp.cumsum(padded)
    pad_starts = pad_ends - padded
    dest_sorted = (pad_starts - starts)[e_sorted] + iota
    _, dest = lax.sort_key_val(order, dest_sorted)
    pos = dest.reshape(TOP_K, T // tc, tc).transpose(1, 0, 2).reshape(-1)

    n_blocks = -(-N // bm) + n_experts
    blk_start = jnp.arange(n_blocks, dtype=i32) * bm
    blk_e = jnp.minimum(jnp.sum((pad_ends[None, :] <= blk_start[:, None]).astype(i32), axis=1), n_experts - 1)
    block_valid = (blk_start < pad_ends[-1]).astype(i32)
    blk_off = blk_start - pad_starts[blk_e]
    block_rows = block_valid * jnp.clip(counts[blk_e] - blk_off, 0, bm)
    prev_e = jnp.concatenate([jnp.full((1,), -1, i32), blk_e[:-1]])
    block_first = block_valid * (blk_e != prev_e).astype(i32)
    last_e = jnp.max(jnp.where(counts > 0, experts, 0))
    block_expert = jnp.where(block_valid > 0, blk_e, last_e)
    later = lax.cummin(jnp.where(counts > 0, experts, n_experts)[::-1])[::-1]
    nxt = jnp.concatenate([later[1:], jnp.full((1,), n_experts, i32)])[block_expert]
    next_ok = (nxt < n_experts).astype(i32)
    next_expert = jnp.minimum(nxt, n_experts - 1)

    in_blk = jnp.arange(bm, dtype=i32)[None, :]
    row_ok = (in_blk < block_rows[:, None]).reshape(-1)
    src = jnp.clip((starts[blk_e] + blk_off)[:, None] + in_blk, 0, N - 1).reshape(-1)
    row_tok = jnp.where(row_ok, (order % T)[src], 0)
    row_gate = jnp.where(row_ok, gate_sorted[src], 0.0)
    return dict(row_tok=row_tok, row_gate=row_gate.reshape(-1, 1), pos=pos, block_expert=block_expert,
                block_valid=block_valid, block_first=block_first, block_rows=block_rows,
                next_expert=next_expert, next_ok=next_ok)


def _tile(n, pref):
    for t in pref:
        if n % t == 0:
            return t
    raise ValueError(f"no tile for {n}")


def _layer(layer, xp, xs, cache_k, cache_v, state_conv, page_table, p):
    n_b, seq, D = xp.shape
    dec_b, dec_seq, _ = xs.shape
    assert n_b == 1
    _, n_phys, page, n_heads, _, head_dim = cache_k.shape
    attn_w = n_heads * 2 * head_dim
    conv_w = p['w_conv'].shape[1]
    assert conv_w == attn_w
    n_exp = p['w_router'].shape[1]
    d_ff = p['w_mlp2'].shape[1]
    Tp, Ts = n_b * seq, dec_b * dec_seq
    T = Tp + Ts
    lam0 = 0.8 - 0.6 * math.exp(-0.3 * layer)

    xp2, xs2 = xp.reshape(Tp, D), xs.reshape(Ts, D)

    tm = _tile(math.gcd(Tp, Ts), (512, 256, 128))
    proj = _in_proj(xp2, xs2, p['norm1_g'], p['w_in'].astype(BF16), tm, 2048)

    tm2 = _tile(math.gcd(Tp, Ts), (256, 128))
    t_idx = jnp.arange(T, dtype=jnp.int32)
    pos_in_seq = jnp.where(t_idx < Tp, t_idx, (t_idx - Tp) % dec_seq)
    m1 = (pos_in_seq >= 1).astype(F32).reshape(T, 1)
    m2 = (pos_in_seq >= 2).astype(F32).reshape(T, 1)
    st = state_conv.astype(F32)
    zero = jnp.zeros((dec_b, dec_seq - 1, conv_w), F32)
    s1 = jnp.concatenate([st[:, 1:2], zero], axis=1).reshape(Ts, conv_w)
    s2 = jnp.concatenate([st[:, 0:2], zero[:, 1:]], axis=1).reshape(Ts, conv_w)
    reps = attn_w // head_dim
    gq = jnp.tile(p['q_norm_g'].astype(F32), reps).reshape(1, attn_w)
    gk = jnp.tile(p['k_norm_g'].astype(F32), reps).reshape(1, attn_w)
    lane = jnp.arange(LANES)
    gmat = ((lane[:, None] // head_dim == lane[None, :] // head_dim).astype(F32) / head_dim).astype(BF16)
    f32 = F32
    slopes = 2.0 ** (-8.0 * jnp.arange(1, n_heads + 1, dtype=f32) / n_heads)
    a_log2 = slopes * LOG2E
    a_hi, a_lo = _split_bf16(a_log2)
    off = jnp.arange(attn_w) % (2 * head_dim)
    a_hi_l, a_lo_l = jnp.repeat(a_hi.astype(f32), 2 * head_dim), jnp.repeat(a_lo.astype(f32), 2 * head_dim)
    aug_row = lambda o: jnp.where(o < 0, 0.0, jnp.where(o < 2, a_hi_l, jnp.where(o < 4, a_lo_l, 0.0)))
    qaug = jnp.stack([aug_row(off - head_dim), aug_row(off)]).reshape(2, 1, attn_w)
    blk = _tile(Tp, (1024, 512, 256, 128))
    ya, u, qa, ka, qs, kf, vf, vb = _mix_prep(proj, m1, m2, s1, s2, p['w_conv'].astype(F32), gq, gk, gmat, qaug,
                                             tm2, Tp // tm2, head_dim, blk)

    lam = (jnp.exp(jnp.sum(p['lambda_q1'].astype(f32) * p['lambda_k1'].astype(f32)))
           - jnp.exp(jnp.sum(p['lambda_q2'].astype(f32) * p['lambda_k2'].astype(f32))) + lam0).reshape(1)
    g_sub = p['subln_g'].astype(F32).reshape(1, 2 * head_dim)
    yb_p = _prompt_attn(a_hi.astype(f32) + a_lo.astype(f32), lam, qa, ka, vb, g_sub, Tp, n_heads, blk, lam0)

    pad_rows = lambda a: jnp.pad(a.reshape(dec_b, dec_seq, attn_w), ((0, 0), (0, QPAD - dec_seq), (0, 0)))
    q_s = pad_rows(qs[Tp:])
    yb_s = _paged_attn(page_table, lam, q_s,
                       jnp.transpose(cache_k, (0, 1, 3, 4, 5, 2)), cache_v, layer,
                       pad_rows(kf[Tp:]), pad_rows(vf[Tp:]), g_sub, dec_seq, n_heads, head_dim, lam0)
    yb = jnp.concatenate([yb_p, yb_s[:, :dec_seq].reshape(Ts, attn_w)], axis=0)

    wr = p['w_router'].astype(F32).T
    wrh, wrl = _split_bf16(wr)
    h, xn2, top_idx, top_gate, expert_counts = _out_proj(
        ya, yb, proj, xp2, xs2, p['w_proj_a'].astype(BF16), p['w_proj_b'].astype(BF16), p['w_o'].astype(BF16),
        p['norm2_g'].astype(F32).reshape(1, D), wrh, wrl, p['b_router'].astype(F32).reshape(n_exp, 1),
        tm2, (3 * conv_w + 3 * attn_w) // D)

    bm, fc, tc = MOE_BM, MOE_FC, MOE_TC
    route = _routing(top_idx, top_gate, expert_counts.reshape(-1).astype(jnp.int32), bm, tc)
    b1g = (p['b_mlp1'].astype(F32).reshape(n_exp, 2 * d_ff // PAIR_W, LANES, 2).transpose(0, 1, 3, 2)
           .reshape(n_exp, 1, 2 * d_ff))
    yr = _moe(route, xn2.reshape(T, D // LANES, LANES), p['w_mlp1'].astype(F32), b1g,
              p['w_mlp2'].astype(F32), p['b_mlp2'].astype(F32).reshape(n_exp, 1, D), bm, fc)
    y_p, y_s = _combine(route['pos'], yr, h, tc, Tp)

    k_p = kf[:Tp].reshape(n_b, seq, n_heads, 2, head_dim)
    v_p = vf[:Tp].reshape(n_b, seq, n_heads, 2 * head_dim)
    c_p = u[Tp - (CONV_K - 1):Tp].reshape(n_b, CONV_K - 1, conv_w)
    k_s = kf[Tp:].reshape(dec_b, dec_seq, n_heads, 2, head_dim)
    v_s = vf[Tp:].reshape(dec_b, dec_seq, n_heads, 2 * head_dim)
    c_s = u[Tp:].reshape(dec_b, dec_seq, conv_w)[:, dec_seq - (CONV_K - 1):]
    return (y_p.reshape(n_b, seq, D), y_s.reshape(dec_b, dec_seq, D), k_p, v_p, c_p, k_s, v_s, c_s)


def kernel(x_prompt, x_sample, cache_k, cache_v, state_conv, page_table, norm1_g, w_in, w_conv, q_norm_g, k_norm_g, lambda_q1, lambda_k1, lambda_q2, lambda_k2, subln_g, w_proj_a, w_proj_b, w_o, norm2_g, w_router, b_router, w_mlp1, b_mlp1, w_mlp2, b_mlp2):
    depth = w_in.shape[0]
    names = ('norm1_g', 'w_in', 'w_conv', 'q_norm_g', 'k_norm_g', 'lambda_q1', 'lambda_k1', 'lambda_q2',
             'lambda_k2', 'subln_g', 'w_proj_a', 'w_proj_b', 'w_o', 'norm2_g', 'w_router', 'b_router',
             'w_mlp1', 'b_mlp1', 'w_mlp2', 'b_mlp2')
    stacked = (norm1_g, w_in, w_conv, q_norm_g, k_norm_g, lambda_q1, lambda_k1, lambda_q2, lambda_k2,
               subln_g, w_proj_a, w_proj_b, w_o, norm2_g, w_router, b_router, w_mlp1, b_mlp1, w_mlp2, b_mlp2)
    h_p, h_s = x_prompt, x_sample
    outs = [[] for _ in range(6)]
    for l in range(depth):
        p = {n: a[l] for n, a in zip(names, stacked)}
        res = _layer(l, h_p, h_s, cache_k, cache_v, state_conv[l], page_table, p)
        h_p, h_s = res[0], res[1]
        for acc, r in zip(outs, res[2:]):
            acc.append(r)
    return (h_p, h_s) + tuple(jnp.stack(o) for o in outs)
```
